```python
import jax, jax.numpy as jnp
from jax import lax
import numpy as np

D_MODEL = 2048
BATCH = 4
SEQ = 4096
DEPTH = 2

N_MIXERS = 2
D_MIX = D_MODEL
MEM_LEN = 256
MEM_HEADS = 4
MEM_HD = 128
MEM_W = MEM_HEADS * MEM_HD
TOK_W = D_MIX - MEM_W
SB_HD = 128
SB_HEADS = TOK_W // SB_HD
SB_BLOCK = 128
SWA_HD = 64
SWA_HEADS = TOK_W // SWA_HD
SWA_GROUP = 8
SWA_KV_HEADS = SWA_HEADS // SWA_GROUP
WINDOW = 128
ROPE_THETA = 10000.0
D_FF = 5632
NORM_EPS = 1e-6
N_SB_LAYERS = (DEPTH + 1) // 2
N_SWA_LAYERS = DEPTH // 2
SB_IN_W = 3 * TOK_W + MEM_W
SWA_IN_W = TOK_W + 2 * SWA_KV_HEADS * SWA_HD + MEM_W

kernel_name = 'hybrid_stickbreaking_swa_sink_macaron_memory'


def rmsnorm(x, g):
    xf = x.astype(jnp.float32)
    y = xf * lax.rsqrt(jnp.mean(xf * xf, axis=-1, keepdims=True) + NORM_EPS)
    return (y * g.astype(jnp.float32)).astype(x.dtype)


def swiglu(x, w_gate, w_up, w_down):
    return (jax.nn.silu(x @ w_gate) * (x @ w_up)) @ w_down


def rope(x, positions):
    half = x.shape[-1] // 2
    inv_freq = ROPE_THETA ** (-jnp.arange(half, dtype=jnp.float32) / half)
    ang = positions.astype(jnp.float32)[..., None] * inv_freq
    cos = jnp.cos(ang)[:, :, None, :]
    sin = jnp.sin(ang)[:, :, None, :]
    x1 = x[..., :half].astype(jnp.float32)
    x2 = x[..., half:].astype(jnp.float32)
    out = jnp.concatenate([x1 * cos - x2 * sin, x2 * cos + x1 * sin], axis=-1)
    return out.astype(x.dtype)


def stick_breaking_attention(q, k, v):
    S = q.shape[1]
    scale = SB_HD ** -0.5
    outs = []
    for i in range(S // SB_BLOCK):
        kv_len = (i + 1) * SB_BLOCK
        q_blk = q[:, i * SB_BLOCK:kv_len]
        k_blk = k[:, :kv_len]
        v_blk = v[:, :kv_len]
        z = jnp.einsum('bthd,bshd->bhts', q_blk, k_blk).astype(jnp.float32) * scale
        t_idx = i * SB_BLOCK + jnp.arange(SB_BLOCK)[:, None]
        s_idx = jnp.arange(kv_len)[None, :]
        causal = s_idx < t_idx
        log1m = jnp.where(causal, -jax.nn.softplus(z), 0.0)
        suffix = lax.cumsum(log1m, axis=3, reverse=True)
        log_a = jax.nn.log_sigmoid(z) + (suffix - log1m)
        a = jnp.where(causal, jnp.exp(log_a), 0.0)
        outs.append(jnp.einsum('bhts,bshd->bthd', a.astype(v.dtype), v_blk))
    return jnp.concatenate(outs, axis=1)


def sliding_window_sink_attention(q, k, v, sinks):
    B, S = q.shape[:2]
    nb = S // WINDOW
    qb = q.reshape(B, nb, WINDOW, SWA_KV_HEADS, SWA_GROUP, SWA_HD)

    def with_prev(t):
        tb = t.reshape(B, nb, WINDOW, SWA_KV_HEADS, SWA_HD)
        prev = jnp.pad(tb, ((0, 0), (1, 0), (0, 0), (0, 0), (0, 0)))[:, :-1]
        return jnp.concatenate([prev, tb], axis=2)

    kb, vb = with_prev(k), with_prev(v)
    z = jnp.einsum('bnqhgd,bnkhd->bnhgqk', qb, kb).astype(jnp.float32) * SWA_HD ** -0.5
    qi = jnp.arange(WINDOW)[:, None]
    ki = jnp.arange(2 * WINDOW)[None, :]
    dist = qi + WINDOW - ki
    key_pos = jnp.arange(nb)[:, None, None] * WINDOW - WINDOW + ki
    valid = (dist >= 0) & (dist < WINDOW) & (key_pos >= 0)
    valid = valid[None, :, None, None]
    z = jnp.where(valid, z, -jnp.inf)
    sink = sinks.astype(jnp.float32).reshape(1, 1, SWA_KV_HEADS, SWA_GROUP, 1, 1)
    m = jnp.maximum(jnp.max(z, axis=-1, keepdims=True), sink)
    p = jnp.exp(z - m)
    p = p / (jnp.sum(p, axis=-1, keepdims=True) + jnp.exp(sink - m))
    out = jnp.einsum('bnhgqk,bnkhd->bnqhgd', p.astype(v.dtype), vb)
    return out.reshape(B, S, SWA_HEADS, SWA_HD)


def memory_attention(qm, mem_n, w_mem_kv, q_g, k_g):
    B, M = mem_n.shape[:2]
    kv = (mem_n @ w_mem_kv).reshape(B, M, 2, MEM_HEADS, MEM_HD)
    km = rmsnorm(kv[:, :, 0], k_g)
    vm = kv[:, :, 1]
    qm = rmsnorm(qm, q_g)
    z = jnp.einsum('bthd,bmhd->bhtm', qm, km).astype(jnp.float32) * MEM_HD ** -0.5
    p = jax.nn.softmax(z, axis=-1)
    return jnp.einsum('bhtm,bmhd->bthd', p.astype(vm.dtype), vm)


def setup_inputs(seed: int = 0) -> dict:
    key = jax.random.key(seed)
    ks = jax.random.split(key, 20)

    def nrm(k, shape, scale):
        return jax.random.normal(k, shape, jnp.float32) * scale

    def gain(k, shape):
        return 1.0 + 0.1 * jax.random.normal(k, shape, jnp.float32)

    x = jax.random.normal(ks[0], (BATCH, SEQ, D_MODEL), jnp.float32)
    mem = jax.random.normal(ks[1], (BATCH, MEM_LEN, D_MODEL), jnp.float32)
    offset = jax.random.randint(ks[2], (BATCH, 1), 0, 1024, jnp.int32)
    positions = offset + jnp.arange(SEQ, dtype=jnp.int32)[None, :]
    return {
        'x': x,
        'mem': mem,
        'positions': positions,
        'ffn_norm_g': gain(ks[3], (DEPTH, 2, D_MODEL)),
        'ffn_w_gate': nrm(ks[4], (DEPTH, 2, D_MODEL, D_FF), D_MODEL ** -0.5),
        'ffn_w_up': nrm(ks[5], (DEPTH, 2, D_MODEL, D_FF), D_MODEL ** -0.5),
        'ffn_w_down': nrm(ks[6], (DEPTH, 2, D_FF, D_MODEL), D_FF ** -0.5),
        'mix_norm_g': gain(ks[7], (DEPTH, D_MODEL)),
        'mem_norm_g': gain(ks[8], (DEPTH, D_MODEL)),
        'w_mem_kv': nrm(ks[9], (DEPTH, D_MODEL, 2 * MEM_W), D_MODEL ** -0.5),
        'memq_norm_g': gain(ks[10], (DEPTH, MEM_HD)),
        'memk_norm_g': gain(ks[11], (DEPTH, MEM_HD)),
        'w_out': nrm(ks[12], (DEPTH, D_MIX, D_MODEL), D_MIX ** -0.5),
        'sb_w_in': nrm(ks[13], (N_SB_LAYERS, D_MODEL, SB_IN_W), D_MODEL ** -0.5),
        'swa_w_in': nrm(ks[14], (N_SWA_LAYERS, D_MODEL, SWA_IN_W), D_MODEL ** -0.5),
        'swa_q_norm_g': gain(ks[15], (N_SWA_LAYERS, SWA_HD)),
        'swa_k_norm_g': gain(ks[16], (N_SWA_LAYERS, SWA_HD)),
        'swa_sinks': nrm(ks[17], (N_SWA_LAYERS, SWA_HEADS), 0.5),
    }


def reference(x, mem, positions, ffn_norm_g, ffn_w_gate, ffn_w_up, ffn_w_down, mix_norm_g, mem_norm_g,
              w_mem_kv, memq_norm_g, memk_norm_g, w_out, sb_w_in, swa_w_in, swa_q_norm_g, swa_k_norm_g,
              swa_sinks):
    B, S, _ = x.shape
    kv_w = SWA_KV_HEADS * SWA_HD
    for i in range(DEPTH):
        h = rmsnorm(x, ffn_norm_g[i, 0])
        x = x + 0.5 * swiglu(h, ffn_w_gate[i, 0], ffn_w_up[i, 0], ffn_w_down[i, 0])

        h = rmsnorm(x, mix_norm_g[i])
        mem_n = rmsnorm(mem, mem_norm_g[i])
        j = i // N_MIXERS
        if i % N_MIXERS == 0:
            proj = h @ sb_w_in[j]
            q, k, v, qm = jnp.split(proj, [TOK_W, 2 * TOK_W, 3 * TOK_W], axis=-1)
            tok = stick_breaking_attention(q.reshape(B, S, SB_HEADS, SB_HD),
                                           k.reshape(B, S, SB_HEADS, SB_HD),
                                           v.reshape(B, S, SB_HEADS, SB_HD))
        else:
            proj = h @ swa_w_in[j]
            q, k, v, qm = jnp.split(proj, [TOK_W, TOK_W + kv_w, TOK_W + 2 * kv_w], axis=-1)
            q = rope(rmsnorm(q.reshape(B, S, SWA_HEADS, SWA_HD), swa_q_norm_g[j]), positions)
            k = rope(rmsnorm(k.reshape(B, S, SWA_KV_HEADS, SWA_HD), swa_k_norm_g[j]), positions)
            tok = sliding_window_sink_attention(q, k, v.reshape(B, S, SWA_KV_HEADS, SWA_HD), swa_sinks[j])
        memo = memory_attention(qm.reshape(B, S, MEM_HEADS, MEM_HD), mem_n, w_mem_kv[i],
                                memq_norm_g[i], memk_norm_g[i])
        mixed = jnp.concatenate([tok.reshape(B, S, TOK_W), memo.reshape(B, S, MEM_W)], axis=-1)
        x = x + mixed @ w_out[i]

        h = rmsnorm(x, ffn_norm_g[i, 1])
        x = x + 0.5 * swiglu(h, ffn_w_gate[i, 1], ffn_w_up[i, 1], ffn_w_down[i, 1])
    return x
```

```python
import functools

import jax
import jax.numpy as jnp
from jax import lax
from jax.experimental import pallas as pl
from jax.experimental.pallas import tpu as pltpu

D_MODEL = 2048
D_FF = 5632
MEM_LEN = 256
MEM_HEADS = 4
MEM_HD = 128
MEM_W = MEM_HEADS * MEM_HD
TOK_W = D_MODEL - MEM_W
SB_HD = 128
SB_HEADS = TOK_W // SB_HD
SWA_HD = 64
SWA_HEADS = TOK_W // SWA_HD
SWA_GROUP = 8
SWA_KV_HEADS = SWA_HEADS // SWA_GROUP
SWA_KV_W = SWA_KV_HEADS * SWA_HD
WINDOW = 128
ROPE_THETA = 10000.0
NORM_EPS = 1e-6

LANES = 128
VMEM_LIMIT = 56 * 1024 * 1024

F32 = jnp.float32
BF16 = jnp.bfloat16
NT_DIMS = (((1,), (1,)), ((), ()))


def _params(semantics):
    return pltpu.CompilerParams(dimension_semantics=semantics, vmem_limit_bytes=VMEM_LIMIT)


def _rms(x, g):
    ms = jnp.mean(x * x, axis=-1, keepdims=True)
    return x * lax.rsqrt(ms + NORM_EPS) * g


def _ffn_kernel(x_ref, g_ref, wg_ref, wu_ref, wd_ref, o_ref, hn_ref):
    @pl.when(pl.program_id(1) == 0)
    def _():
        x = x_ref[...]
        hn_ref[...] = _rms(x, g_ref[...]).astype(BF16)
        o_ref[...] = x

    h = hn_ref[...]
    gate = jnp.dot(h, wg_ref[...], preferred_element_type=F32)
    up = jnp.dot(h, wu_ref[...], preferred_element_type=F32)
    act = (gate * jax.nn.sigmoid(gate)) * up * 0.5
    o_ref[...] += jnp.dot(act.astype(BF16), wd_ref[...], preferred_element_type=F32)


def _ffn(x, g, wg, wu, wd, *, tm=1024, tf=512):
    n, d = x.shape
    f = wg.shape[1]
    return pl.pallas_call(
        _ffn_kernel,
        out_shape=jax.ShapeDtypeStruct((n, d), F32),
        grid=(n // tm, f // tf),
        in_specs=[
            pl.BlockSpec((tm, d), lambda i, j: (i, 0)),
            pl.BlockSpec((1, d), lambda i, j: (0, 0)),
            pl.BlockSpec((d, tf), lambda i, j: (0, j)),
            pl.BlockSpec((d, tf), lambda i, j: (0, j)),
            pl.BlockSpec((tf, d), lambda i, j: (j, 0)),
        ],
        out_specs=pl.BlockSpec((tm, d), lambda i, j: (i, 0)),
        scratch_shapes=[pltpu.VMEM((tm, d), BF16)],
        compiler_params=_params(("parallel", "arbitrary")),
        name="ffn",
    )(x, g.reshape(1, d), wg, wu, wd)


def _norm_proj_kernel(x_ref, g_ref, w_ref, o_ref, hn_ref):
    @pl.when(pl.program_id(1) == 0)
    def _():
        hn_ref[...] = _rms(x_ref[...], g_ref[...]).astype(BF16)

    o_ref[...] = jnp.dot(hn_ref[...], w_ref[...], preferred_element_type=F32).astype(o_ref.dtype)


def _norm_proj(x, g, w, *, tm=1024, tn):
    n, d = x.shape
    wout = w.shape[1]
    return pl.pallas_call(
        _norm_proj_kernel,
        out_shape=jax.ShapeDtypeStruct((n, wout), BF16),
        grid=(n // tm, wout // tn),
        in_specs=[
            pl.BlockSpec((tm, d), lambda i, j: (i, 0)),
            pl.BlockSpec((1, d), lambda i, j: (0, 0)),
            pl.BlockSpec((d, tn), lambda i, j: (0, j)),
        ],
        out_specs=pl.BlockSpec((tm, tn), lambda i, j: (i, j)),
        scratch_shapes=[pltpu.VMEM((tm, d), BF16)],
        compiler_params=_params(("parallel", "arbitrary")),
        name="norm_proj",
    )(x, g.reshape(1, d), w)


def _mem_kv_kernel(mem_ref, g_ref, w_ref, kg_ref, km_ref, vm_ref):
    mem_n = _rms(mem_ref[...], g_ref[...]).astype(BF16)
    kv = jnp.dot(mem_n, w_ref[...], preferred_element_type=F32)
    for h in range(MEM_HEADS):
        cols = slice(h * MEM_HD, (h + 1) * MEM_HD)
        km_ref[:, cols] = _rms(kv[:, cols], kg_ref[...]).astype(BF16)
    vm_ref[...] = kv[:, MEM_W:].astype(BF16)


def _mem_kv(mem, g, w, kg):
    b, m, d = mem.shape
    out = jax.ShapeDtypeStruct((b, m, MEM_W), BF16)
    return pl.pallas_call(
        _mem_kv_kernel,
        out_shape=(out, out),
        grid=(b,),
        in_specs=[
            pl.BlockSpec((None, m, d), lambda i: (i, 0, 0)),
            pl.BlockSpec((1, d), lambda i: (0, 0)),
            pl.BlockSpec((d, 2 * MEM_W), lambda i: (0, 0)),
            pl.BlockSpec((1, MEM_HD), lambda i: (0, 0)),
        ],
        out_specs=(pl.BlockSpec((None, m, MEM_W), lambda i: (i, 0, 0)),
                   pl.BlockSpec((None, m, MEM_W), lambda i: (i, 0, 0))),
        compiler_params=_params(("parallel",)),
        name="mem_kv",
    )(mem, g.reshape(1, d), w, kg.reshape(1, MEM_HD))


def _split_bf16(x):
    hi = x.astype(BF16)
    lo = (x - hi.astype(F32)).astype(BF16)
    return hi, lo


def _sb_kernel(q_ref, k_ref, v_ref, o_ref, *, blk):
    qi = pl.program_id(2)
    q = q_ref[...]
    scale = SB_HD ** -0.5
    row = lax.broadcasted_iota(jnp.int32, (blk, blk), 0)
    col = lax.broadcasted_iota(jnp.int32, (blk, blk), 1)
    suffix_ones = (row >= col).astype(BF16)
    causal = col < row

    def block(start, carry, acc, masked):
        k = k_ref[pl.ds(start, blk), :]
        v = v_ref[pl.ds(start, blk), :]
        z = lax.dot_general(q, k, NT_DIMS, preferred_element_type=F32) * scale
        log1m = -(jnp.maximum(z, 0.0) + jnp.log1p(jnp.exp(-jnp.abs(z))))
        if masked:
            log1m = jnp.where(causal, log1m, 0.0)
        hi, lo = _split_bf16(log1m)
        incl = (jnp.dot(hi, suffix_ones, preferred_element_type=F32)
                + jnp.dot(lo, suffix_ones, preferred_element_type=F32))
        a = jnp.exp(z + incl + carry)
        if masked:
            a = jnp.where(causal, a, 0.0)
        acc = acc + jnp.dot(a.astype(BF16), v, preferred_element_type=F32)
        return carry + incl[:, 0:1], acc

    carry0 = jnp.zeros((blk, 1), F32)
    acc0 = jnp.zeros((blk, SB_HD), F32)
    carry, acc = block(pl.multiple_of(qi * blk, blk), carry0, acc0, True)

    def body(i, state):
        start = pl.multiple_of((qi - 1 - i) * blk, blk)
        return block(start, state[0], state[1], False)

    _, acc = lax.fori_loop(0, qi, body, (carry, acc))
    o_ref[...] = acc.astype(o_ref.dtype)


def _sb_attention(proj, *, blk=128):
    b, s, _ = proj.shape
    return pl.pallas_call(
        functools.partial(_sb_kernel, blk=blk),
        out_shape=jax.ShapeDtypeStruct((b, s, TOK_W), BF16),
        grid=(b, SB_HEADS, s // blk),
        in_specs=[
            pl.BlockSpec((None, blk, SB_HD), lambda bi, h, i: (bi, i, h)),
            pl.BlockSpec((None, s, SB_HD), lambda bi, h, i: (bi, 0, SB_HEADS + h)),
            pl.BlockSpec((None, s, SB_HD), lambda bi, h, i: (bi, 0, 2 * SB_HEADS + h)),
        ],
        out_specs=pl.BlockSpec((None, blk, SB_HD), lambda bi, h, i: (bi, i, h)),
        compiler_params=_params(("parallel", "parallel", "arbitrary")),
        name="sb_attention",
    )(proj, proj, proj)


def _rope(x, cos, sin_signed, first_half):
    partner = jnp.where(first_half, pltpu.roll(x, 96, axis=1), pltpu.roll(x, 32, axis=1))
    return x * cos + partner * sin_signed


def _head_rms(x, pair_ones, g):
    hi, lo = _split_bf16(x * x)
    ss = (jnp.dot(hi, pair_ones, preferred_element_type=F32)
          + jnp.dot(lo, pair_ones, preferred_element_type=F32))
    return x * lax.rsqrt(ss * (1.0 / SWA_HD) + NORM_EPS) * g


def _swa_kernel(sinks_ref, q_ref, kvc_ref, kvp_ref, posc_ref, posp_ref, freq_ref, qg_ref, kg_ref,
                o_ref):
    n = pl.program_id(1)
    w = WINDOW
    lane = lax.broadcasted_iota(jnp.int32, (1, LANES), 1)
    first_half = (lane & (SWA_HD - 1)) < (SWA_HD // 2)
    low = lane < SWA_HD
    r = lax.broadcasted_iota(jnp.int32, (LANES, LANES), 0)
    c = lax.broadcasted_iota(jnp.int32, (LANES, LANES), 1)
    pair_ones = ((r < SWA_HD) == (c < SWA_HD)).astype(BF16)

    def tables(pos_ref):
        ang = pos_ref[...].astype(F32) * freq_ref[...]
        sin = jnp.sin(ang)
        return jnp.cos(ang), jnp.where(first_half, -sin, sin)

    cos_c, sin_c = tables(posc_ref)
    cos_p, sin_p = tables(posp_ref)
    cos_k = jnp.concatenate([cos_p, cos_c], axis=0)
    sin_k = jnp.concatenate([sin_p, sin_c], axis=0)

    n_chunks = TOK_W // LANES
    q = q_ref[...].astype(F32)
    qs = jnp.concatenate([q[:, i * LANES:(i + 1) * LANES] for i in range(n_chunks)], axis=0)
    qs = _head_rms(qs, pair_ones, qg_ref[...])
    qs = _rope(qs, jnp.concatenate([cos_c] * n_chunks, axis=0),
               jnp.concatenate([sin_c] * n_chunks, axis=0), first_half).astype(BF16)

    kv = jnp.concatenate([kvp_ref[...], kvc_ref[...]], axis=0).astype(F32)
    kn = [_rope(_head_rms(kv[:, i * LANES:(i + 1) * LANES], pair_ones, kg_ref[...]),
                cos_k, sin_k, first_half) for i in range(2)]

    def block_diag(x, in_low):
        sw = pltpu.roll(x, SWA_HD, axis=1)
        top = jnp.where(low, x if in_low else sw, 0.0)
        bot = jnp.where(low, 0.0, sw if in_low else x)
        return jnp.concatenate([top, bot], axis=0).astype(BF16)

    k_src = [(kn[0], True), (kn[0], False), (kn[1], True)]
    v_src = [(kv[:, LANES:2 * LANES], False), (kv[:, 2 * LANES:], True), (kv[:, 2 * LANES:], False)]

    qpos = lax.broadcasted_iota(jnp.int32, (w, 2 * w), 0)
    kpos = lax.broadcasted_iota(jnp.int32, (w, 2 * w), 1)
    dist = qpos + w - kpos
    valid = (dist >= 0) & (dist < w) & ((kpos >= w) | (n > 0))

    for h in range(SWA_KV_HEADS):
        kd = block_diag(*k_src[h])
        vd = block_diag(*v_src[h])
        qh = qs[h * 4 * w:(h + 1) * 4 * w]
        z = lax.dot_general(qh, kd, NT_DIMS, preferred_element_type=F32) * (SWA_HD ** -0.5)
        rows = []
        for ci in range(4):
            halves = []
            for p in range(2):
                sink = sinks_ref[h * SWA_GROUP + 2 * ci + p]
                zb = jnp.where(valid, z[ci * w:(ci + 1) * w, p * 2 * w:(p + 1) * 2 * w], -jnp.inf)
                m = jnp.maximum(jnp.max(zb, axis=-1, keepdims=True), sink)
                e = jnp.exp(zb - m)
                denom = jnp.sum(e, axis=-1, keepdims=True) + jnp.exp(sink - m)
                halves.append((e / denom).astype(BF16))
            rows.append(jnp.concatenate(halves, axis=1))
        p_all = jnp.concatenate(rows, axis=0)
        out = jnp.dot(p_all, vd, preferred_element_type=F32)
        for ci in range(4):
            cols = slice((h * 4 + ci) * LANES, (h * 4 + ci + 1) * LANES)
            o_ref[:, cols] = out[ci * w:(ci + 1) * w].astype(o_ref.dtype)


def _swa_attention(proj, positions, q_gain, k_gain, sinks):
    b, s, _ = proj.shape
    w = WINDOW
    kv_blk = TOK_W // (2 * SWA_KV_W)
    half = SWA_HD // 2
    inv_freq = ROPE_THETA ** (-jnp.arange(half, dtype=F32) / half)
    freq = jnp.tile(inv_freq, LANES // half).reshape(1, LANES)
    pos = positions.reshape(b, s, 1)
    grid_spec = pltpu.PrefetchScalarGridSpec(
        num_scalar_prefetch=1,
        grid=(b, s // w),
        in_specs=[
            pl.BlockSpec((None, w, TOK_W), lambda bi, i, sk: (bi, i, 0)),
            pl.BlockSpec((None, w, 2 * SWA_KV_W), lambda bi, i, sk: (bi, i, kv_blk)),
            pl.BlockSpec((None, w, 2 * SWA_KV_W), lambda bi, i, sk: (bi, jnp.maximum(i - 1, 0), kv_blk)),
            pl.BlockSpec((None, w, 1), lambda bi, i, sk: (bi, i, 0)),
            pl.BlockSpec((None, w, 1), lambda bi, i, sk: (bi, jnp.maximum(i - 1, 0), 0)),
            pl.BlockSpec((1, LANES), lambda bi, i, sk: (0, 0)),
            pl.BlockSpec((1, LANES), lambda bi, i, sk: (0, 0)),
            pl.BlockSpec((1, LANES), lambda bi, i, sk: (0, 0)),
        ],
        out_specs=pl.BlockSpec((None, w, TOK_W), lambda bi, i, sk: (bi, i, 0)),
    )
    return pl.pallas_call(
        _swa_kernel,
        out_shape=jax.ShapeDtypeStruct((b, s, TOK_W), BF16),
        grid_spec=grid_spec,
        compiler_params=_params(("parallel", "arbitrary")),
        name="swa_attention",
    )(sinks.astype(F32), proj, proj, proj, pos, pos, freq,
      jnp.tile(q_gain, 2).reshape(1, LANES), jnp.tile(k_gain, 2).reshape(1, LANES))


def _mix_out_kernel(x_ref, tok_ref, q0_ref, q1_ref, q2_ref, q3_ref, km_ref, vm_ref, qg_ref, wo_ref,
                    o_ref):
    acc = x_ref[...] + jnp.dot(tok_ref[...], wo_ref[:TOK_W, :], preferred_element_type=F32)
    memo = []
    for h, q_ref in enumerate((q0_ref, q1_ref, q2_ref, q3_ref)):
        cols = slice(h * MEM_HD, (h + 1) * MEM_HD)
        qn = _rms(q_ref[...].astype(F32), qg_ref[...]).astype(BF16)
        z = lax.dot_general(qn, km_ref[:, cols], NT_DIMS, preferred_element_type=F32) * (MEM_HD ** -0.5)
        e = jnp.exp(z - jnp.max(z, axis=-1, keepdims=True))
        p = e / jnp.sum(e, axis=-1, keepdims=True)
        memo.append(jnp.dot(p.astype(BF16), vm_ref[:, cols], preferred_element_type=F32).astype(BF16))
    memo = jnp.concatenate(memo, axis=1)
    o_ref[...] = acc + jnp.dot(memo, wo_ref[TOK_W:, :], preferred_element_type=F32)


def _mix_out(x, tok, proj, km, vm, q_gain, w_out, *, memq_col, tm=512):
    b, s, d = x.shape
    qblk = memq_col // MEM_HD
    q_specs = [pl.BlockSpec((None, tm, MEM_HD), functools.partial(lambda bi, i, h: (bi, i, qblk + h), h=h))
               for h in range(MEM_HEADS)]
    return pl.pallas_call(
        _mix_out_kernel,
        out_shape=jax.ShapeDtypeStruct((b, s, d), F32),
        grid=(b, s // tm),
        in_specs=[
            pl.BlockSpec((None, tm, d), lambda bi, i: (bi, i, 0)),
            pl.BlockSpec((None, tm, TOK_W), lambda bi, i: (bi, i, 0)),
            *q_specs,
            pl.BlockSpec((None, MEM_LEN, MEM_W), lambda bi, i: (bi, 0, 0)),
            pl.BlockSpec((None, MEM_LEN, MEM_W), lambda bi, i: (bi, 0, 0)),
            pl.BlockSpec((1, MEM_HD), lambda bi, i: (0, 0)),
            pl.BlockSpec((d, d), lambda bi, i: (0, 0)),
        ],
        out_specs=pl.BlockSpec((None, tm, d), lambda bi, i: (bi, i, 0)),
        compiler_params=_params(("parallel", "arbitrary")),
        name="mix_out",
    )(x, tok, proj, proj, proj, proj, km, vm, q_gain.reshape(1, MEM_HD), w_out)


def kernel(x, mem, positions, ffn_norm_g, ffn_w_gate, ffn_w_up, ffn_w_down, mix_norm_g, mem_norm_g,
           w_mem_kv, memq_norm_g, memk_norm_g, w_out, sb_w_in, swa_w_in, swa_q_norm_g, swa_k_norm_g,
           swa_sinks):
    b, s, d = x.shape
    depth = ffn_norm_g.shape[0]
    for i in range(depth):
        x2 = _ffn(x.reshape(b * s, d), ffn_norm_g[i, 0], ffn_w_gate[i, 0].astype(BF16),
                  ffn_w_up[i, 0].astype(BF16), ffn_w_down[i, 0].astype(BF16))
        km, vm = _mem_kv(mem, mem_norm_g[i], w_mem_kv[i].astype(BF16), memk_norm_g[i])
        j = i // 2
        if i % 2 == 0:
            proj = _norm_proj(x2, mix_norm_g[i], sb_w_in[j].astype(BF16), tn=1024).reshape(b, s, -1)
            tok = _sb_attention(proj)
            memq_col = 3 * TOK_W
        else:
            w_in = swa_w_in[j].astype(BF16)
            proj = _norm_proj(x2, mix_norm_g[i], w_in, tn=w_in.shape[1]).reshape(b, s, -1)
            tok = _swa_attention(proj, positions, swa_q_norm_g[j], swa_k_norm_g[j], swa_sinks[j])
            memq_col = TOK_W + 2 * SWA_KV_W
        x3 = _mix_out(x2.reshape(b, s, d), tok, proj, km, vm, memq_norm_g[i], w_out[i].astype(BF16),
                      memq_col=memq_col)
        x = _ffn(x3.reshape(b * s, d), ffn_norm_g[i, 1], ffn_w_gate[i, 1].astype(BF16),
                 ffn_w_up[i, 1].astype(BF16), ffn_w_down[i, 1].astype(BF16)).reshape(b, s, d)
    return x
```

```python
import functools

import jax
import jax.numpy as jnp
from jax import lax
from jax.experimental import pallas as pl
from jax.experimental.pallas import tpu as pltpu

D_MODEL = 2048
D_FF = 5632
MEM_LEN = 256
MEM_HEADS = 4
MEM_HD = 128
MEM_W = MEM_HEADS * MEM_HD
TOK_W = D_MODEL - MEM_W
SB_HD = 128
SB_HEADS = TOK_W // SB_HD
SWA_HD = 64
SWA_HEADS = TOK_W // SWA_HD
SWA_GROUP = 8
SWA_KV_HEADS = SWA_HEADS // SWA_GROUP
SWA_KV_W = SWA_KV_HEADS * SWA_HD
WINDOW = 128
ROPE_THETA = 10000.0
NORM_EPS = 1e-6

LANES = 128
VMEM_LIMIT = 56 * 1024 * 1024

F32 = jnp.float32
BF16 = jnp.bfloat16
NT_DIMS = (((1,), (1,)), ((), ()))


def _params(semantics, flags=None):
    return pltpu.CompilerParams(dimension_semantics=semantics, vmem_limit_bytes=VMEM_LIMIT,
                                flags=flags)


def _rms(x, g):
    ms = jnp.mean(x * x, axis=-1, keepdims=True)
    return x * lax.rsqrt(ms + NORM_EPS) * g


def _ffn_kernel(x_ref, g_ref, wg_ref, wu_ref, wd_ref, o_ref, hn_ref):
    @pl.when(pl.program_id(1) == 0)
    def _():
        x = x_ref[...]
        hn_ref[...] = _rms(x, g_ref[...]).astype(BF16)
        o_ref[...] = x

    h = hn_ref[...]
    gate = jnp.dot(h, wg_ref[...], preferred_element_type=F32)
    up = jnp.dot(h, wu_ref[...], preferred_element_type=F32)
    act = (gate * jax.nn.sigmoid(gate)) * up * 0.5
    o_ref[...] += jnp.dot(act.astype(BF16), wd_ref[...], preferred_element_type=F32)


def _ffn(x, g, wg, wu, wd, *, tm=1024, tf=512):
    n, d = x.shape
    f = wg.shape[1]
    return pl.pallas_call(
        _ffn_kernel,
        out_shape=jax.ShapeDtypeStruct((n, d), F32),
        grid=(n // tm, f // tf),
        in_specs=[
            pl.BlockSpec((tm, d), lambda i, j: (i, 0)),
            pl.BlockSpec((1, d), lambda i, j: (0, 0)),
            pl.BlockSpec((d, tf), lambda i, j: (0, j)),
            pl.BlockSpec((d, tf), lambda i, j: (0, j)),
            pl.BlockSpec((tf, d), lambda i, j: (j, 0)),
        ],
        out_specs=pl.BlockSpec((tm, d), lambda i, j: (i, 0)),
        scratch_shapes=[pltpu.VMEM((tm, d), BF16)],
        compiler_params=_params(("parallel", "arbitrary")),
        name="ffn",
    )(x, g.reshape(1, d), wg, wu, wd)


def _norm_proj_kernel(x_ref, g_ref, w_ref, o_ref, hn_ref):
    @pl.when(pl.program_id(1) == 0)
    def _():
        hn_ref[...] = _rms(x_ref[...], g_ref[...]).astype(BF16)

    o_ref[...] = jnp.dot(hn_ref[...], w_ref[...], preferred_element_type=F32).astype(o_ref.dtype)


def _norm_proj(x, g, w, *, tm=1024, tn):
    n, d = x.shape
    wout = w.shape[1]
    return pl.pallas_call(
        _norm_proj_kernel,
        out_shape=jax.ShapeDtypeStruct((n, wout), BF16),
        grid=(n // tm, wout // tn),
        in_specs=[
            pl.BlockSpec((tm, d), lambda i, j: (i, 0)),
            pl.BlockSpec((1, d), lambda i, j: (0, 0)),
            pl.BlockSpec((d, tn), lambda i, j: (0, j)),
        ],
        out_specs=pl.BlockSpec((tm, tn), lambda i, j: (i, j)),
        scratch_shapes=[pltpu.VMEM((tm, d), BF16)],
        compiler_params=_params(("parallel", "arbitrary")),
        name="norm_proj",
    )(x, g.reshape(1, d), w)


def _mem_kv_kernel(mem_ref, g_ref, w_ref, kg_ref, km_ref, vm_ref):
    mem_n = _rms(mem_ref[...], g_ref[...]).astype(BF16)
    kv = jnp.dot(mem_n, w_ref[...], preferred_element_type=F32)
    for h in range(MEM_HEADS):
        cols = slice(h * MEM_HD, (h + 1) * MEM_HD)
        km_ref[:, cols] = _rms(kv[:, cols], kg_ref[...]).astype(BF16)
    vm_ref[...] = kv[:, MEM_W:].astype(BF16)


def _mem_kv(mem, g, w, kg):
    b, m, d = mem.shape
    out = jax.ShapeDtypeStruct((b, m, MEM_W), BF16)
    return pl.pallas_call(
        _mem_kv_kernel,
        out_shape=(out, out),
        grid=(b,),
        in_specs=[
            pl.BlockSpec((None, m, d), lambda i: (i, 0, 0)),
            pl.BlockSpec((1, d), lambda i: (0, 0)),
            pl.BlockSpec((d, 2 * MEM_W), lambda i: (0, 0)),
            pl.BlockSpec((1, MEM_HD), lambda i: (0, 0)),
        ],
        out_specs=(pl.BlockSpec((None, m, MEM_W), lambda i: (i, 0, 0)),
                   pl.BlockSpec((None, m, MEM_W), lambda i: (i, 0, 0))),
        compiler_params=_params(("parallel",)),
        name="mem_kv",
    )(mem, g.reshape(1, d), w, kg.reshape(1, MEM_HD))


def _split_bf16(x):
    hi = x.astype(BF16)
    lo = (x - hi.astype(F32)).astype(BF16)
    return hi, lo


def _neg_abs(x):
    bits = pltpu.bitcast(x, jnp.uint32) | jnp.uint32(0x80000000)
    return pltpu.bitcast(bits, F32)


def _sb_kernel(q_ref, k_ref, vt_ref, o_ref, *, blk, qw, heads):
    qi = pl.program_id(2)
    diag_blocks = qw // blk
    half = blk // 2
    row = lax.broadcasted_iota(jnp.int32, (half, half), 0)
    col = lax.broadcasted_iota(jnp.int32, (half, half), 1)
    suffix_ones = (col >= row).astype(BF16)
    suffix_ones2 = jnp.concatenate([suffix_ones, suffix_ones], axis=1)
    key_idx = lax.broadcasted_iota(jnp.int32, (half, qw), 0)
    query_idx = lax.broadcasted_iota(jnp.int32, (half, qw), 1)
    hs = range(heads)
    cols = [slice(h * SB_HD, (h + 1) * SB_HD) for h in hs]
    rows = [slice(0, half), slice(half, blk)]
    qs = [q_ref[:, cols[h]] for h in hs]

    def block(j, state, key_offset):
        start = pl.multiple_of(j * blk, blk)
        masked = key_offset is not None
        if masked:
            causal = [(key_idx + (key_offset + p * half)) < query_idx for p in range(2)]
        zs = [lax.dot_general(k_ref[pl.ds(start, blk), cols[h]], qs[h], NT_DIMS,
                              preferred_element_type=F32) for h in hs]
        zs = [[z[rows[p]] for p in range(2)] for z in zs]

        def suffix_sum(z, p):
            sp = jnp.maximum(z, 0.0) + jnp.log(1.0 + jnp.exp(_neg_abs(z)))
            if masked:
                sp = jnp.where(causal[p], sp, 0.0)
            return jnp.dot(suffix_ones2, jnp.concatenate(_split_bf16(sp), axis=0),
                           preferred_element_type=F32)

        incls = [[suffix_sum(zh[p], p) for p in range(2)] for zh in zs]
        new_state = []
        for h in hs:
            carry, acc = state[h]
            late = jnp.exp(zs[h][1] - incls[h][1] - carry)
            carry = carry + incls[h][1][0:1, :]
            early = jnp.exp(zs[h][0] - incls[h][0] - carry)
            carry = carry + incls[h][0][0:1, :]
            if masked:
                early = jnp.where(causal[0], early, 0.0)
                late = jnp.where(causal[1], late, 0.0)
            probs = jnp.concatenate([early.astype(BF16), late.astype(BF16)], axis=0)
            acc = acc + jnp.dot(vt_ref[j, cols[h], :], probs, preferred_element_type=F32)
            new_state.append((carry, acc))
        return tuple(new_state)

    state = tuple((jnp.zeros((1, qw), F32), jnp.zeros((SB_HD, qw), F32)) for _ in hs)
    first = qi * diag_blocks
    for r in reversed(range(diag_blocks)):
        state = block(first + r, state, r * blk)
    state = lax.fori_loop(0, first, lambda i, st: block(first - 1 - i, st, None), state)
    for h in hs:
        o_ref[:, cols[h]] = state[h][1].T.astype(o_ref.dtype)


def _sb_attention(proj, *, blk=256, qw=512, heads=4):
    b, s, _ = proj.shape
    gw = heads * SB_HD
    groups = TOK_W // gw
    vt = proj[:, :, 2 * TOK_W:3 * TOK_W].reshape(b, s // blk, blk, TOK_W).swapaxes(2, 3)
    return pl.pallas_call(
        functools.partial(_sb_kernel, blk=blk, qw=qw, heads=heads),
        out_shape=jax.ShapeDtypeStruct((b, s, TOK_W), BF16),
        grid=(b, groups, s // qw),
        in_specs=[
            pl.BlockSpec((None, qw, gw), lambda bi, g, i: (bi, i, g)),
            pl.BlockSpec((None, s, gw), lambda bi, g, i: (bi, 0, groups + g)),
            pl.BlockSpec((None, s // blk, gw, blk), lambda bi, g, i: (bi, 0, g, 0)),
        ],
        out_specs=pl.BlockSpec((None, qw, gw), lambda bi, g, i: (bi, i, g)),
        compiler_params=_params(("parallel", "parallel", "arbitrary")),
        name="sb_attention",
    )(proj, proj, vt)


def _rope(x, cos, sin_signed, first_half):
    partner = jnp.where(first_half, pltpu.roll(x, 96, axis=1), pltpu.roll(x, 32, axis=1))
    return x * cos + partner * sin_signed


def _head_rms(x, pair_ones, g):
    hi, lo = _split_bf16(x * x)
    ss = (jnp.dot(hi, pair_ones, preferred_element_type=F32)
          + jnp.dot(lo, pair_ones, preferred_element_type=F32))
    return x * lax.rsqrt(ss * (1.0 / SWA_HD) + NORM_EPS) * g


def _swa_kernel(sinks_ref, q_ref, kvc_ref, kvp_ref, posc_ref, posp_ref, freq_ref, qg_ref, kg_ref,
                o_ref):
    n = pl.program_id(1)
    w = WINDOW
    lane = lax.broadcasted_iota(jnp.int32, (1, LANES), 1)
    first_half = (lane & (SWA_HD - 1)) < (SWA_HD // 2)
    low = lane < SWA_HD
    r = lax.broadcasted_iota(jnp.int32, (LANES, LANES), 0)
    c = lax.broadcasted_iota(jnp.int32, (LANES, LANES), 1)
    pair_ones = ((r < SWA_HD) == (c < SWA_HD)).astype(BF16)

    def tables(pos_ref):
        ang = pos_ref[...].astype(F32) * freq_ref[...]
        sin = jnp.sin(ang)
        return jnp.cos(ang), jnp.where(first_half, -sin, sin)

    cos_c, sin_c = tables(posc_ref)
    cos_p, sin_p = tables(posp_ref)
    cos_k = jnp.concatenate([cos_p, cos_c], axis=0)
    sin_k = jnp.concatenate([sin_p, sin_c], axis=0)

    n_chunks = TOK_W // LANES
    q = q_ref[...].astype(F32)
    qs = jnp.concatenate([q[:, i * LANES:(i + 1) * LANES] for i in range(n_chunks)], axis=0)
    qs = _head_rms(qs, pair_ones, qg_ref[...])
    qs = _rope(qs, jnp.concatenate([cos_c] * n_chunks, axis=0),
               jnp.concatenate([sin_c] * n_chunks, axis=0), first_half).astype(BF16)

    kv = jnp.concatenate([kvp_ref[...], kvc_ref[...]], axis=0).astype(F32)
    kn = [_rope(_head_rms(kv[:, i * LANES:(i + 1) * LANES], pair_ones, kg_ref[...]),
                cos_k, sin_k, first_half) for i in range(2)]

    def block_diag(x, in_low):
        sw = pltpu.roll(x, SWA_HD, axis=1)
        top = jnp.where(low, x if in_low else sw, 0.0)
        bot = jnp.where(low, 0.0, sw if in_low else x)
        return jnp.concatenate([top, bot], axis=0).astype(BF16)

    k_src = [(kn[0], True), (kn[0], False), (kn[1], True)]
    v_src = [(kv[:, LANES:2 * LANES], False), (kv[:, 2 * LANES:], True), (kv[:, 2 * LANES:], False)]

    qpos = lax.broadcasted_iota(jnp.int32, (w, 2 * w), 0)
    kpos = lax.broadcasted_iota(jnp.int32, (w, 2 * w), 1)
    dist = qpos + w - kpos
    valid = (dist >= 0) & (dist < w) & ((kpos >= w) | (n > 0))

    for h in range(SWA_KV_HEADS):
        kd = block_diag(*k_src[h])
        vd = block_diag(*v_src[h])
        qh = qs[h * 4 * w:(h + 1) * 4 * w]
        z = lax.dot_general(qh, kd, NT_DIMS, preferred_element_type=F32) * (SWA_HD ** -0.5)
        rows = []
        for ci in range(4):
            halves = []
            for p in range(2):
                sink = sinks_ref[h * SWA_GROUP + 2 * ci + p]
                zb = jnp.where(valid, z[ci * w:(ci + 1) * w, p * 2 * w:(p + 1) * 2 * w], -jnp.inf)
                m = jnp.maximum(jnp.max(zb, axis=-1, keepdims=True), sink)
                e = jnp.exp(zb - m)
                denom = jnp.sum(e, axis=-1, keepdims=True) + jnp.exp(sink - m)
                halves.append((e / denom).astype(BF16))
            rows.append(jnp.concatenate(halves, axis=1))
        p_all = jnp.concatenate(rows, axis=0)
        out = jnp.dot(p_all, vd, preferred_element_type=F32)
        for ci in range(4):
            cols = slice((h * 4 + ci) * LANES, (h * 4 + ci + 1) * LANES)
            o_ref[:, cols] = out[ci * w:(ci + 1) * w].astype(o_ref.dtype)


def _swa_attention(proj, positions, q_gain, k_gain, sinks):
    b, s, _ = proj.shape
    w = WINDOW
    kv_blk = TOK_W // (2 * SWA_KV_W)
    half = SWA_HD // 2
    inv_freq = ROPE_THETA ** (-jnp.arange(half, dtype=F32) / half)
    freq = jnp.tile(inv_freq, LANES // half).reshape(1, LANES)
    pos = positions.reshape(b, s, 1)
    grid_spec = pltpu.PrefetchScalarGridSpec(
        num_scalar_prefetch=1,
        grid=(b, s // w),
        in_specs=[
            pl.BlockSpec((None, w, TOK_W), lambda bi, i, sk: (bi, i, 0)),
            pl.BlockSpec((None, w, 2 * SWA_KV_W), lambda bi, i, sk: (bi, i, kv_blk)),
            pl.BlockSpec((None, w, 2 * SWA_KV_W), lambda bi, i, sk: (bi, jnp.maximum(i - 1, 0), kv_blk)),
            pl.BlockSpec((None, w, 1), lambda bi, i, sk: (bi, i, 0)),
            pl.BlockSpec((None, w, 1), lambda bi, i, sk: (bi, jnp.maximum(i - 1, 0), 0)),
            pl.BlockSpec((1, LANES), lambda bi, i, sk: (0, 0)),
            pl.BlockSpec((1, LANES), lambda bi, i, sk: (0, 0)),
            pl.BlockSpec((1, LANES), lambda bi, i, sk: (0, 0)),
        ],
        out_specs=pl.BlockSpec((None, w, TOK_W), lambda bi, i, sk: (bi, i, 0)),
    )
    return pl.pallas_call(
        _swa_kernel,
        out_shape=jax.ShapeDtypeStruct((b, s, TOK_W), BF16),
        grid_spec=grid_spec,
        compiler_params=_params(("parallel", "arbitrary")),
        name="swa_attention",
    )(sinks.astype(F32), proj, proj, proj, pos, pos, freq,
      jnp.tile(q_gain, 2).reshape(1, LANES), jnp.tile(k_gain, 2).reshape(1, LANES))


def _mix_out_kernel(x_ref, tok_ref, q0_ref, q1_ref, q2_ref, q3_ref, km_ref, vm_ref, qg_ref, wo_ref,
                    o_ref):
    acc = x_ref[...] + jnp.dot(tok_ref[...], wo_ref[:TOK_W, :], preferred_element_type=F32)
    memo = []
    for h, q_ref in enumerate((q0_ref, q1_ref, q2_ref, q3_ref)):
        cols = slice(h * MEM_HD, (h + 1) * MEM_HD)
        qn = _rms(q_ref[...].astype(F32), qg_ref[...]).astype(BF16)
        z = lax.dot_general(qn, km_ref[:, cols], NT_DIMS, preferred_element_type=F32) * (MEM_HD ** -0.5)
        e = jnp.exp(z - jnp.max(z, axis=-1, keepdims=True))
        p = e / jnp.sum(e, axis=-1, keepdims=True)
        memo.append(jnp.dot(p.astype(BF16), vm_ref[:, cols], preferred_element_type=F32).astype(BF16))
    memo = jnp.concatenate(memo, axis=1)
    o_ref[...] = acc + jnp.dot(memo, wo_ref[TOK_W:, :], preferred_element_type=F32)


def _mix_out(x, tok, proj, km, vm, q_gain, w_out, *, memq_col, tm=512):
    b, s, d = x.shape
    qblk = memq_col // MEM_HD
    q_specs = [pl.BlockSpec((None, tm, MEM_HD), functools.partial(lambda bi, i, h: (bi, i, qblk + h), h=h))
               for h in range(MEM_HEADS)]
    return pl.pallas_call(
        _mix_out_kernel,
        out_shape=jax.ShapeDtypeStruct((b, s, d), F32),
        grid=(b, s // tm),
        in_specs=[
            pl.BlockSpec((None, tm, d), lambda bi, i: (bi, i, 0)),
            pl.BlockSpec((None, tm, TOK_W), lambda bi, i: (bi, i, 0)),
            *q_specs,
            pl.BlockSpec((None, MEM_LEN, MEM_W), lambda bi, i: (bi, 0, 0)),
            pl.BlockSpec((None, MEM_LEN, MEM_W), lambda bi, i: (bi, 0, 0)),
            pl.BlockSpec((1, MEM_HD), lambda bi, i: (0, 0)),
            pl.BlockSpec((d, d), lambda bi, i: (0, 0)),
        ],
        out_specs=pl.BlockSpec((None, tm, d), lambda bi, i: (bi, i, 0)),
        compiler_params=_params(("parallel", "arbitrary")),
        name="mix_out",
    )(x, tok, proj, proj, proj, proj, km, vm, q_gain.reshape(1, MEM_HD), w_out)


def kernel(x, mem, positions, ffn_norm_g, ffn_w_gate, ffn_w_up, ffn_w_down, mix_norm_g, mem_norm_g,
           w_mem_kv, memq_norm_g, memk_norm_g, w_out, sb_w_in, swa_w_in, swa_q_norm_g, swa_k_norm_g,
           swa_sinks):
    b, s, d = x.shape
    depth = ffn_norm_g.shape[0]
    for i in range(depth):
        x2 = _ffn(x.reshape(b * s, d), ffn_norm_g[i, 0], ffn_w_gate[i, 0].astype(BF16),
                  ffn_w_up[i, 0].astype(BF16), ffn_w_down[i, 0].astype(BF16))
        km, vm = _mem_kv(mem, mem_norm_g[i], w_mem_kv[i].astype(BF16), memk_norm_g[i])
        j = i // 2
        if i % 2 == 0:
            col_scale = jnp.where(jnp.arange(sb_w_in.shape[2]) < TOK_W, SB_HD ** -0.5, 1.0)
            w_in = (sb_w_in[j] * col_scale).astype(BF16)
            proj = _norm_proj(x2, mix_norm_g[i], w_in, tn=1024).reshape(b, s, -1)
            tok = _sb_attention(proj)
            memq_col = 3 * TOK_W
        else:
            w_in = swa_w_in[j].astype(BF16)
            proj = _norm_proj(x2, mix_norm_g[i], w_in, tn=w_in.shape[1]).reshape(b, s, -1)
            tok = _swa_attention(proj, positions, swa_q_norm_g[j], swa_k_norm_g[j], swa_sinks[j])
            memq_col = TOK_W + 2 * SWA_KV_W
        x3 = _mix_out(x2.reshape(b, s, d), tok, proj, km, vm, memq_norm_g[i], w_out[i].astype(BF16),
                      memq_col=memq_col)
        x = _ffn(x3.reshape(b * s, d), ffn_norm_g[i, 1], ffn_w_gate[i, 1].astype(BF16),
                 ffn_w_up[i, 1].astype(BF16), ffn_w_down[i, 1].astype(BF16)).reshape(b, s, d)
    return x
```

```python
import functools

import jax
import jax.numpy as jnp
from jax import lax
from jax.experimental import pallas as pl
from jax.experimental.pallas import tpu as pltpu

D_MODEL = 2048
D_FF = 5632
MEM_LEN = 256
MEM_HEADS = 4
MEM_HD = 128
MEM_W = MEM_HEADS * MEM_HD
TOK_W = D_MODEL - MEM_W
SB_HD = 128
SB_HEADS = TOK_W // SB_HD
SWA_HD = 64
SWA_HEADS = TOK_W // SWA_HD
SWA_GROUP = 8
SWA_KV_HEADS = SWA_HEADS // SWA_GROUP
SWA_KV_W = SWA_KV_HEADS * SWA_HD
WINDOW = 128
ROPE_THETA = 10000.0
NORM_EPS = 1e-6

LANES = 128
VMEM_LIMIT = 56 * 1024 * 1024

F32 = jnp.float32
BF16 = jnp.bfloat16
NT_DIMS = (((1,), (1,)), ((), ()))


def _params(semantics, flags=None):
    return pltpu.CompilerParams(dimension_semantics=semantics, vmem_limit_bytes=VMEM_LIMIT,
                                flags=flags)


def _rms(x, g):
    ms = jnp.mean(x * x, axis=-1, keepdims=True)
    return x * lax.rsqrt(ms + NORM_EPS) * g


def _ffn_kernel(x_ref, g_ref, wg_ref, wu_ref, wd_ref, o_ref, hn_ref):
    @pl.when(pl.program_id(1) == 0)
    def _():
        x = x_ref[...]
        hn_ref[...] = _rms(x, g_ref[...]).astype(BF16)
        o_ref[...] = x

    h = hn_ref[...]
    gate = jnp.dot(h, wg_ref[...], preferred_element_type=F32)
    up = jnp.dot(h, wu_ref[...], preferred_element_type=F32)
    act = (gate * jax.nn.sigmoid(gate)) * up * 0.5
    o_ref[...] += jnp.dot(act.astype(BF16), wd_ref[...], preferred_element_type=F32)


def _ffn(x, g, wg, wu, wd, layer, half, *, tm=1024, tf=512):
    n, d = x.shape
    f = wg.shape[-1]
    return pl.pallas_call(
        _ffn_kernel,
        out_shape=jax.ShapeDtypeStruct((n, d), F32),
        grid=(n // tm, f // tf),
        in_specs=[
            pl.BlockSpec((tm, d), lambda i, j: (i, 0)),
            pl.BlockSpec((1, d), lambda i, j: (0, 0)),
            pl.BlockSpec((None, None, d, tf), lambda i, j: (layer, half, 0, j)),
            pl.BlockSpec((None, None, d, tf), lambda i, j: (layer, half, 0, j)),
            pl.BlockSpec((None, None, tf, d), lambda i, j: (layer, half, j, 0)),
        ],
        out_specs=pl.BlockSpec((tm, d), lambda i, j: (i, 0)),
        scratch_shapes=[pltpu.VMEM((tm, d), BF16)],
        compiler_params=_params(("parallel", "arbitrary")),
        name="ffn",
    )(x, g.reshape(1, d), wg, wu, wd)


def _norm_proj_kernel(x_ref, g_ref, w_ref, o_ref, hn_ref):
    @pl.when(pl.program_id(1) == 0)
    def _():
        hn_ref[...] = _rms(x_ref[...], g_ref[...]).astype(BF16)

    o_ref[...] = jnp.dot(hn_ref[...], w_ref[...], preferred_element_type=F32).astype(o_ref.dtype)


def _norm_proj(x, g, w, *, tm=1024, tn):
    n, d = x.shape
    wout = w.shape[1]
    return pl.pallas_call(
        _norm_proj_kernel,
        out_shape=jax.ShapeDtypeStruct((n, wout), BF16),
        grid=(n // tm, wout // tn),
        in_specs=[
            pl.BlockSpec((tm, d), lambda i, j: (i, 0)),
            pl.BlockSpec((1, d), lambda i, j: (0, 0)),
            pl.BlockSpec((d, tn), lambda i, j: (0, j)),
        ],
        out_specs=pl.BlockSpec((tm, tn), lambda i, j: (i, j)),
        scratch_shapes=[pltpu.VMEM((tm, d), BF16)],
        compiler_params=_params(("parallel", "arbitrary")),
        name="norm_proj",
    )(x, g.reshape(1, d), w)


def _mem_kv_kernel(mem_ref, g_ref, w_ref, kg_ref, km_ref, vm_ref):
    mem_n = _rms(mem_ref[...], g_ref[...]).astype(BF16)
    kv = jnp.dot(mem_n, w_ref[...], preferred_element_type=F32)
    for h in range(MEM_HEADS):
        cols = slice(h * MEM_HD, (h + 1) * MEM_HD)
        km_ref[:, cols] = _rms(kv[:, cols], kg_ref[...]).astype(BF16)
    vm_ref[...] = kv[:, MEM_W:].astype(BF16)


def _mem_kv(mem, g, w, kg, layer):
    b, m, d = mem.shape
    out = jax.ShapeDtypeStruct((b, m, MEM_W), BF16)
    return pl.pallas_call(
        _mem_kv_kernel,
        out_shape=(out, out),
        grid=(b,),
        in_specs=[
            pl.BlockSpec((None, m, d), lambda i: (i, 0, 0)),
            pl.BlockSpec((1, d), lambda i: (0, 0)),
            pl.BlockSpec((None, d, 2 * MEM_W), lambda i: (layer, 0, 0)),
            pl.BlockSpec((1, MEM_HD), lambda i: (0, 0)),
        ],
        out_specs=(pl.BlockSpec((None, m, MEM_W), lambda i: (i, 0, 0)),
                   pl.BlockSpec((None, m, MEM_W), lambda i: (i, 0, 0))),
        compiler_params=_params(("parallel",)),
        name="mem_kv",
    )(mem, g.reshape(1, d), w, kg.reshape(1, MEM_HD))


def _split_bf16(x):
    hi = x.astype(BF16)
    lo = (x - hi.astype(F32)).astype(BF16)
    return hi, lo


def _sb_kernel(q_ref, k_ref, vt_ref, o_ref, *, blk, qw, heads):
    qi = pl.program_id(2)
    diag_blocks = qw // blk
    half = blk // 2
    row = lax.broadcasted_iota(jnp.int32, (half, half), 0)
    col = lax.broadcasted_iota(jnp.int32, (half, half), 1)
    suffix_ones = (col >= row).astype(BF16)
    suffix_ones2 = jnp.concatenate([suffix_ones, suffix_ones], axis=1)
    key_idx = lax.broadcasted_iota(jnp.int32, (half, qw), 0)
    query_idx = lax.broadcasted_iota(jnp.int32, (half, qw), 1)
    hs = range(heads)
    cols = [slice(h * SB_HD, (h + 1) * SB_HD) for h in hs]
    rows = [slice(0, half), slice(half, blk)]
    qs = [q_ref[:, cols[h]] for h in hs]

    def block(j, state, key_offset):
        start = pl.multiple_of(j * blk, blk)
        masked = key_offset is not None
        if masked:
            causal = [(key_idx + (key_offset + p * half)) < query_idx for p in range(2)]
        zs = [lax.dot_general(k_ref[pl.ds(start, blk), cols[h]], qs[h], NT_DIMS,
                              preferred_element_type=F32) for h in hs]
        zs = [[z[rows[p]] for p in range(2)] for z in zs]

        def suffix_sum(z, p):
            sp = jnp.maximum(z, 0.0) + jnp.log(1.0 + jnp.exp(-jnp.abs(z)))
            if masked:
                sp = jnp.where(causal[p], sp, 0.0)
            return jnp.dot(suffix_ones2, jnp.concatenate(_split_bf16(sp), axis=0),
                           preferred_element_type=F32)

        incls = [[suffix_sum(zh[p], p) for p in range(2)] for zh in zs]
        new_state = []
        for h in hs:
            carry, acc = state[h]
            late = jnp.exp(zs[h][1] - incls[h][1] - carry)
            carry = carry + incls[h][1][0:1, :]
            early = jnp.exp(zs[h][0] - incls[h][0] - carry)
            carry = carry + incls[h][0][0:1, :]
            if masked:
                early = jnp.where(causal[0], early, 0.0)
                late = jnp.where(causal[1], late, 0.0)
            probs = jnp.concatenate([early.astype(BF16), late.astype(BF16)], axis=0)
            acc = acc + jnp.dot(vt_ref[j, cols[h], :], probs, preferred_element_type=F32)
            new_state.append((carry, acc))
        return tuple(new_state)

    state = tuple((jnp.zeros((1, qw), F32), jnp.zeros((SB_HD, qw), F32)) for _ in hs)
    first = qi * diag_blocks
    for r in reversed(range(diag_blocks)):
        state = block(first + r, state, r * blk)
    state = lax.fori_loop(0, first, lambda i, st: block(first - 1 - i, st, None), state)
    for h in hs:
        o_ref[:, cols[h]] = state[h][1].T.astype(o_ref.dtype)


def _sb_attention(proj, *, blk=256, qw=512, heads=4):
    b, s, _ = proj.shape
    gw = heads * SB_HD
    groups = TOK_W // gw
    vt = proj[:, :, 2 * TOK_W:3 * TOK_W].reshape(b, s // blk, blk, TOK_W).swapaxes(2, 3)
    return pl.pallas_call(
        functools.partial(_sb_kernel, blk=blk, qw=qw, heads=heads),
        out_shape=jax.ShapeDtypeStruct((b, s, TOK_W), BF16),
        grid=(b, groups, s // qw),
        in_specs=[
            pl.BlockSpec((None, qw, gw), lambda bi, g, i: (bi, i, g)),
            pl.BlockSpec((None, s, gw), lambda bi, g, i: (bi, 0, groups + g)),
            pl.BlockSpec((None, s // blk, gw, blk), lambda bi, g, i: (bi, 0, g, 0)),
        ],
        out_specs=pl.BlockSpec((None, qw, gw), lambda bi, g, i: (bi, i, g)),
        compiler_params=_params(("parallel", "parallel", "arbitrary")),
        name="sb_attention",
    )(proj, proj, vt)


def _rope(x, cos, sin_signed, first_half):
    partner = jnp.where(first_half, pltpu.roll(x, 96, axis=1), pltpu.roll(x, 32, axis=1))
    return x * cos + partner * sin_signed


def _head_rms(x, pair_ones, g):
    hi, lo = _split_bf16(x * x)
    ss = (jnp.dot(hi, pair_ones, preferred_element_type=F32)
          + jnp.dot(lo, pair_ones, preferred_element_type=F32))
    return x * lax.rsqrt(ss * (1.0 / SWA_HD) + NORM_EPS) * g


def _swa_kernel(sinks_ref, q_ref, kvc_ref, kvp_ref, posc_ref, posp_ref, freq_ref, qg_ref, kg_ref,
                o_ref):
    n = pl.program_id(1)
    w = WINDOW
    lane = lax.broadcasted_iota(jnp.int32, (1, LANES), 1)
    first_half = (lane & (SWA_HD - 1)) < (SWA_HD // 2)
    low = lane < SWA_HD
    r = lax.broadcasted_iota(jnp.int32, (LANES, LANES), 0)
    c = lax.broadcasted_iota(jnp.int32, (LANES, LANES), 1)
    pair_ones = ((r < SWA_HD) == (c < SWA_HD)).astype(BF16)

    def tables(pos_ref):
        ang = pos_ref[...].astype(F32) * freq_ref[...]
        sin = jnp.sin(ang)
        return jnp.cos(ang), jnp.where(first_half, -sin, sin)

    cos_c, sin_c = tables(posc_ref)
    cos_p, sin_p = tables(posp_ref)
    cos_k = jnp.concatenate([cos_p, cos_c], axis=0)
    sin_k = jnp.concatenate([sin_p, sin_c], axis=0)

    n_chunks = TOK_W // LANES
    q = q_ref[...].astype(F32)
    qs = jnp.concatenate([q[:, i * LANES:(i + 1) * LANES] for i in range(n_chunks)], axis=0)
    qs = _head_rms(qs, pair_ones, qg_ref[...])
    qs = _rope(qs, jnp.concatenate([cos_c] * n_chunks, axis=0),
               jnp.concatenate([sin_c] * n_chunks, axis=0), first_half).astype(BF16)

    kv = jnp.concatenate([kvp_ref[...], kvc_ref[...]], axis=0).astype(F32)
    kn = [_rope(_head_rms(kv[:, i * LANES:(i + 1) * LANES], pair_ones, kg_ref[...]),
                cos_k, sin_k, first_half) for i in range(2)]

    def block_diag(x, in_low):
        sw = pltpu.roll(x, SWA_HD, axis=1)
        top = jnp.where(low, x if in_low else sw, 0.0)
        bot = jnp.where(low, 0.0, sw if in_low else x)
        return jnp.concatenate([top, bot], axis=0).astype(BF16)

    k_src = [(kn[0], True), (kn[0], False), (kn[1], True)]
    v_src = [(kv[:, LANES:2 * LANES], False), (kv[:, 2 * LANES:], True), (kv[:, 2 * LANES:], False)]

    qpos = lax.broadcasted_iota(jnp.int32, (w, 2 * w), 0)
    kpos = lax.broadcasted_iota(jnp.int32, (w, 2 * w), 1)
    dist = qpos + w - kpos
    valid = (dist >= 0) & (dist < w) & ((kpos >= w) | (n > 0))

    for h in range(SWA_KV_HEADS):
        kd = block_diag(*k_src[h])
        vd = block_diag(*v_src[h])
        qh = qs[h * 4 * w:(h + 1) * 4 * w]
        z = lax.dot_general(qh, kd, NT_DIMS, preferred_element_type=F32) * (SWA_HD ** -0.5)
        rows = []
        for ci in range(4):
            halves = []
            for p in range(2):
                sink = sinks_ref[h * SWA_GROUP + 2 * ci + p]
                zb = jnp.where(valid, z[ci * w:(ci + 1) * w, p * 2 * w:(p + 1) * 2 * w], -jnp.inf)
                m = jnp.maximum(jnp.max(zb, axis=-1, keepdims=True), sink)
                e = jnp.exp(zb - m)
                denom = jnp.sum(e, axis=-1, keepdims=True) + jnp.exp(sink - m)
                halves.append((e / denom).astype(BF16))
            rows.append(jnp.concatenate(halves, axis=1))
        p_all = jnp.concatenate(rows, axis=0)
        out = jnp.dot(p_all, vd, preferred_element_type=F32)
        for ci in range(4):
            cols = slice((h * 4 + ci) * LANES, (h * 4 + ci + 1) * LANES)
            o_ref[:, cols] = out[ci * w:(ci + 1) * w].astype(o_ref.dtype)


def _swa_attention(proj, positions, q_gain, k_gain, sinks):
    b, s, _ = proj.shape
    w = WINDOW
    kv_blk = TOK_W // (2 * SWA_KV_W)
    half = SWA_HD // 2
    inv_freq = ROPE_THETA ** (-jnp.arange(half, dtype=F32) / half)
    freq = jnp.tile(inv_freq, LANES // half).reshape(1, LANES)
    pos = positions.reshape(b, s, 1)
    grid_spec = pltpu.PrefetchScalarGridSpec(
        num_scalar_prefetch=1,
        grid=(b, s // w),
        in_specs=[
            pl.BlockSpec((None, w, TOK_W), lambda bi, i, sk: (bi, i, 0)),
            pl.BlockSpec((None, w, 2 * SWA_KV_W), lambda bi, i, sk: (bi, i, kv_blk)),
            pl.BlockSpec((None, w, 2 * SWA_KV_W), lambda bi, i, sk: (bi, jnp.maximum(i - 1, 0), kv_blk)),
            pl.BlockSpec((None, w, 1), lambda bi, i, sk: (bi, i, 0)),
            pl.BlockSpec((None, w, 1), lambda bi, i, sk: (bi, jnp.maximum(i - 1, 0), 0)),
            pl.BlockSpec((1, LANES), lambda bi, i, sk: (0, 0)),
            pl.BlockSpec((1, LANES), lambda bi, i, sk: (0, 0)),
            pl.BlockSpec((1, LANES), lambda bi, i, sk: (0, 0)),
        ],
        out_specs=pl.BlockSpec((None, w, TOK_W), lambda bi, i, sk: (bi, i, 0)),
    )
    return pl.pallas_call(
        _swa_kernel,
        out_shape=jax.ShapeDtypeStruct((b, s, TOK_W), BF16),
        grid_spec=grid_spec,
        compiler_params=_params(("parallel", "arbitrary")),
        name="swa_attention",
    )(sinks.astype(F32), proj, proj, proj, pos, pos, freq,
      jnp.tile(q_gain, 2).reshape(1, LANES), jnp.tile(k_gain, 2).reshape(1, LANES))


def _mix_out_kernel(x_ref, tok_ref, q0_ref, q1_ref, q2_ref, q3_ref, km_ref, vm_ref, qg_ref, wo_ref,
                    o_ref):
    acc = x_ref[...] + jnp.dot(tok_ref[...], wo_ref[:TOK_W, :], preferred_element_type=F32)
    memo = []
    for h, q_ref in enumerate((q0_ref, q1_ref, q2_ref, q3_ref)):
        cols = slice(h * MEM_HD, (h + 1) * MEM_HD)
        qn = _rms(q_ref[...].astype(F32), qg_ref[...]).astype(BF16)
        z = lax.dot_general(qn, km_ref[:, cols], NT_DIMS, preferred_element_type=F32) * (MEM_HD ** -0.5)
        e = jnp.exp(z - jnp.max(z, axis=-1, keepdims=True))
        p = e / jnp.sum(e, axis=-1, keepdims=True)
        memo.append(jnp.dot(p.astype(BF16), vm_ref[:, cols], preferred_element_type=F32).astype(BF16))
    memo = jnp.concatenate(memo, axis=1)
    o_ref[...] = acc + jnp.dot(memo, wo_ref[TOK_W:, :], preferred_element_type=F32)


def _mix_out(x, tok, proj, km, vm, q_gain, w_out, layer, *, memq_col, tm=512):
    b, s, d = x.shape
    qblk = memq_col // MEM_HD
    q_specs = [pl.BlockSpec((None, tm, MEM_HD), functools.partial(lambda bi, i, h: (bi, i, qblk + h), h=h))
               for h in range(MEM_HEADS)]
    return pl.pallas_call(
        _mix_out_kernel,
        out_shape=jax.ShapeDtypeStruct((b, s, d), F32),
        grid=(b, s // tm),
        in_specs=[
            pl.BlockSpec((None, tm, d), lambda bi, i: (bi, i, 0)),
            pl.BlockSpec((None, tm, TOK_W), lambda bi, i: (bi, i, 0)),
            *q_specs,
            pl.BlockSpec((None, MEM_LEN, MEM_W), lambda bi, i: (bi, 0, 0)),
            pl.BlockSpec((None, MEM_LEN, MEM_W), lambda bi, i: (bi, 0, 0)),
            pl.BlockSpec((1, MEM_HD), lambda bi, i: (0, 0)),
            pl.BlockSpec((None, d, d), lambda bi, i: (layer, 0, 0)),
        ],
        out_specs=pl.BlockSpec((None, tm, d), lambda bi, i: (bi, i, 0)),
        compiler_params=_params(("parallel", "arbitrary")),
        name="mix_out",
    )(x, tok, proj, proj, proj, proj, km, vm, q_gain.reshape(1, MEM_HD), w_out)


def kernel(x, mem, positions, ffn_norm_g, ffn_w_gate, ffn_w_up, ffn_w_down, mix_norm_g, mem_norm_g,
           w_mem_kv, memq_norm_g, memk_norm_g, w_out, sb_w_in, swa_w_in, swa_q_norm_g, swa_k_norm_g,
           swa_sinks):
    b, s, d = x.shape
    depth = ffn_norm_g.shape[0]
    wg, wu, wd = ffn_w_gate.astype(BF16), ffn_w_up.astype(BF16), ffn_w_down.astype(BF16)
    w_kv, w_o = w_mem_kv.astype(BF16), w_out.astype(BF16)
    for i in range(depth):
        x2 = _ffn(x.reshape(b * s, d), ffn_norm_g[i, 0], wg, wu, wd, i, 0)
        km, vm = _mem_kv(mem, mem_norm_g[i], w_kv, memk_norm_g[i], i)
        j = i // 2
        if i % 2 == 0:
            col_scale = jnp.where(jnp.arange(sb_w_in.shape[2]) < TOK_W, SB_HD ** -0.5, 1.0)
            w_in = (sb_w_in[j] * col_scale).astype(BF16)
            proj = _norm_proj(x2, mix_norm_g[i], w_in, tn=1024).reshape(b, s, -1)
            tok = _sb_attention(proj)
            memq_col = 3 * TOK_W
        else:
            w_in = swa_w_in[j].astype(BF16)
            proj = _norm_proj(x2, mix_norm_g[i], w_in, tn=w_in.shape[1]).reshape(b, s, -1)
            tok = _swa_attention(proj, positions, swa_q_norm_g[j], swa_k_norm_g[j], swa_sinks[j])
            memq_col = TOK_W + 2 * SWA_KV_W
        x3 = _mix_out(x2.reshape(b, s, d), tok, proj, km, vm, memq_norm_g[i], w_o, i,
                      memq_col=memq_col)
        x = _ffn(x3.reshape(b * s, d), ffn_norm_g[i, 1], wg, wu, wd, i, 1).reshape(b, s, d)
    return x
```

```python
import functools

import jax
import jax.numpy as jnp
from jax import lax
from jax.experimental import pallas as pl
from jax.experimental.pallas import tpu as pltpu

D_MODEL = 2048
D_FF = 5632
MEM_LEN = 256
MEM_HEADS = 4
MEM_HD = 128
MEM_W = MEM_HEADS * MEM_HD
TOK_W = D_MODEL - MEM_W
SB_HD = 128
SB_HEADS = TOK_W // SB_HD
SWA_HD = 64
SWA_HEADS = TOK_W // SWA_HD
SWA_GROUP = 8
SWA_KV_HEADS = SWA_HEADS // SWA_GROUP
SWA_KV_W = SWA_KV_HEADS * SWA_HD
WINDOW = 128
ROPE_THETA = 10000.0
NORM_EPS = 1e-6

LANES = 128
VMEM_LIMIT = 56 * 1024 * 1024

F32 = jnp.float32
BF16 = jnp.bfloat16
NT_DIMS = (((1,), (1,)), ((), ()))


def _params(semantics, flags=None):
    return pltpu.CompilerParams(dimension_semantics=semantics, vmem_limit_bytes=VMEM_LIMIT,
                                flags=flags)


def _rms(x, g):
    ms = jnp.mean(x * x, axis=-1, keepdims=True)
    return x * lax.rsqrt(ms + NORM_EPS) * g


def _ffn_kernel(x_ref, g_ref, wg_ref, wu_ref, wd_ref, o_ref, hn_ref):
    @pl.when(pl.program_id(1) == 0)
    def _():
        x = x_ref[...]
        hn_ref[...] = _rms(x, g_ref[...]).astype(BF16)
        o_ref[...] = x

    h = hn_ref[...]
    gate = jnp.dot(h, wg_ref[...], preferred_element_type=F32)
    up = jnp.dot(h, wu_ref[...], preferred_element_type=F32)
    act = (gate * jax.nn.sigmoid(gate)) * up * 0.5
    o_ref[...] += jnp.dot(act.astype(BF16), wd_ref[...], preferred_element_type=F32)


def _ffn(x, g, wg, wu, wd, layer, half, *, tm=1024, tf=512):
    n, d = x.shape
    f = wg.shape[-1]
    return pl.pallas_call(
        _ffn_kernel,
        out_shape=jax.ShapeDtypeStruct((n, d), F32),
        grid=(n // tm, f // tf),
        in_specs=[
            pl.BlockSpec((tm, d), lambda i, j: (i, 0)),
            pl.BlockSpec((1, d), lambda i, j: (0, 0)),
            pl.BlockSpec((None, None, d, tf), lambda i, j: (layer, half, 0, j)),
            pl.BlockSpec((None, None, d, tf), lambda i, j: (layer, half, 0, j)),
            pl.BlockSpec((None, None, tf, d), lambda i, j: (layer, half, j, 0)),
        ],
        out_specs=pl.BlockSpec((tm, d), lambda i, j: (i, 0)),
        scratch_shapes=[pltpu.VMEM((tm, d), BF16)],
        compiler_params=_params(("parallel", "arbitrary")),
        name="ffn",
    )(x, g.reshape(1, d), wg, wu, wd)


def _norm_proj_kernel(x_ref, g_ref, w_ref, o_ref, hn_ref):
    @pl.when(pl.program_id(1) == 0)
    def _():
        hn_ref[...] = _rms(x_ref[...], g_ref[...]).astype(BF16)

    o_ref[...] = jnp.dot(hn_ref[...], w_ref[...], preferred_element_type=F32).astype(o_ref.dtype)


def _norm_proj(x, g, w, *, tm=1024, tn):
    n, d = x.shape
    wout = w.shape[1]
    return pl.pallas_call(
        _norm_proj_kernel,
        out_shape=jax.ShapeDtypeStruct((n, wout), BF16),
        grid=(n // tm, wout // tn),
        in_specs=[
            pl.BlockSpec((tm, d), lambda i, j: (i, 0)),
            pl.BlockSpec((1, d), lambda i, j: (0, 0)),
            pl.BlockSpec((d, tn), lambda i, j: (0, j)),
        ],
        out_specs=pl.BlockSpec((tm, tn), lambda i, j: (i, j)),
        scratch_shapes=[pltpu.VMEM((tm, d), BF16)],
        compiler_params=_params(("parallel", "arbitrary")),
        name="norm_proj",
    )(x, g.reshape(1, d), w)


def _mem_kv_kernel(mem_ref, g_ref, w_ref, kg_ref, km_ref, vm_ref):
    mem_n = _rms(mem_ref[...], g_ref[...]).astype(BF16)
    kv = jnp.dot(mem_n, w_ref[...], preferred_element_type=F32)
    for h in range(MEM_HEADS):
        cols = slice(h * MEM_HD, (h + 1) * MEM_HD)
        km_ref[:, cols] = _rms(kv[:, cols], kg_ref[...]).astype(BF16)
    vm_ref[...] = kv[:, MEM_W:].astype(BF16)


def _mem_kv(mem, g, w, kg, layer):
    b, m, d = mem.shape
    out = jax.ShapeDtypeStruct((b, m, MEM_W), BF16)
    return pl.pallas_call(
        _mem_kv_kernel,
        out_shape=(out, out),
        grid=(b,),
        in_specs=[
            pl.BlockSpec((None, m, d), lambda i: (i, 0, 0)),
            pl.BlockSpec((1, d), lambda i: (0, 0)),
            pl.BlockSpec((None, d, 2 * MEM_W), lambda i: (layer, 0, 0)),
            pl.BlockSpec((1, MEM_HD), lambda i: (0, 0)),
        ],
        out_specs=(pl.BlockSpec((None, m, MEM_W), lambda i: (i, 0, 0)),
                   pl.BlockSpec((None, m, MEM_W), lambda i: (i, 0, 0))),
        compiler_params=_params(("parallel",)),
        name="mem_kv",
    )(mem, g.reshape(1, d), w, kg.reshape(1, MEM_HD))


SB_CARRY_DONE = 110.0


def _split_bf16(x):
    hi = x.astype(BF16)
    lo = (x - hi.astype(F32)).astype(BF16)
    return hi, lo


def _sb_kernel(q_ref, k_ref, vt_ref, o_ref, *, blk, qw, heads):
    qi = pl.program_id(2)
    diag_blocks = qw // blk
    half = blk // 2
    row = lax.broadcasted_iota(jnp.int32, (half, half), 0)
    col = lax.broadcasted_iota(jnp.int32, (half, half), 1)
    suffix_ones = (col >= row).astype(BF16)
    suffix_ones2 = jnp.concatenate([suffix_ones, suffix_ones], axis=1)
    key_idx = lax.broadcasted_iota(jnp.int32, (half, qw), 0)
    query_idx = lax.broadcasted_iota(jnp.int32, (half, qw), 1)
    hs = range(heads)
    cols = [slice(h * SB_HD, (h + 1) * SB_HD) for h in hs]
    rows = [slice(0, half), slice(half, blk)]
    qs = [q_ref[:, cols[h]] for h in hs]

    def block(j, state, key_offset):
        start = pl.multiple_of(j * blk, blk)
        masked = key_offset is not None
        if masked:
            causal = [(key_idx + (key_offset + p * half)) < query_idx for p in range(2)]
        zs = [lax.dot_general(k_ref[pl.ds(start, blk), cols[h]], qs[h], NT_DIMS,
                              preferred_element_type=F32) for h in hs]
        zs = [[z[rows[p]] for p in range(2)] for z in zs]

        def suffix_sum(z, p):
            sp = jnp.maximum(z, 0.0) + jnp.log(1.0 + jnp.exp(-jnp.abs(z)))
            if masked:
                sp = jnp.where(causal[p], sp, 0.0)
            return jnp.dot(suffix_ones2, jnp.concatenate(_split_bf16(sp), axis=0),
                           preferred_element_type=F32)

        incls = [[suffix_sum(zh[p], p) for p in range(2)] for zh in zs]
        new_state = []
        for h in hs:
            carry, acc = state[h]
            late = jnp.exp(zs[h][1] - incls[h][1] - carry)
            carry = carry + incls[h][1][0:1, :]
            early = jnp.exp(zs[h][0] - incls[h][0] - carry)
            carry = carry + incls[h][0][0:1, :]
            if masked:
                early = jnp.where(causal[0], early, 0.0)
                late = jnp.where(causal[1], late, 0.0)
            probs = jnp.concatenate([early.astype(BF16), late.astype(BF16)], axis=0)
            acc = acc + jnp.dot(vt_ref[j, cols[h], :], probs, preferred_element_type=F32)
            new_state.append((carry, acc))
        return tuple(new_state)

    state = tuple((jnp.zeros((1, qw), F32), jnp.zeros((SB_HD, qw), F32)) for _ in hs)
    first = qi * diag_blocks
    for r in reversed(range(diag_blocks)):
        state = block(first + r, state, r * blk)

    def unfinished(st):
        least = functools.reduce(jnp.minimum, [st[h][0] for h in hs])
        return (jnp.min(least) < SB_CARRY_DONE).astype(jnp.int32)

    def sweep(c):
        i, _, st = c
        st = block(first - 1 - i, st, None)
        return i + 1, unfinished(st), st

    _, _, state = lax.while_loop(lambda c: jnp.logical_and(c[0] < first, c[1] > 0), sweep,
                                 (jnp.int32(0), unfinished(state), state))
    for h in hs:
        o_ref[:, cols[h]] = state[h][1].T.astype(o_ref.dtype)


def _sb_attention(proj, *, blk=256, qw=512, heads=4):
    b, s, _ = proj.shape
    gw = heads * SB_HD
    groups = TOK_W // gw
    vt = proj[:, :, 2 * TOK_W:3 * TOK_W].reshape(b, s // blk, blk, TOK_W).swapaxes(2, 3)
    return pl.pallas_call(
        functools.partial(_sb_kernel, blk=blk, qw=qw, heads=heads),
        out_shape=jax.ShapeDtypeStruct((b, s, TOK_W), BF16),
        grid=(b, groups, s // qw),
        in_specs=[
            pl.BlockSpec((None, qw, gw), lambda bi, g, i: (bi, i, g)),
            pl.BlockSpec((None, s, gw), lambda bi, g, i: (bi, 0, groups + g)),
            pl.BlockSpec((None, s // blk, gw, blk), lambda bi, g, i: (bi, 0, g, 0)),
        ],
        out_specs=pl.BlockSpec((None, qw, gw), lambda bi, g, i: (bi, i, g)),
        compiler_params=_params(("parallel", "parallel", "arbitrary")),
        name="sb_attention",
    )(proj, proj, vt)


def _rope(x, cos, sin_signed, first_half):
    partner = jnp.where(first_half, pltpu.roll(x, 96, axis=1), pltpu.roll(x, 32, axis=1))
    return x * cos + partner * sin_signed


def _head_rms(x, pair_ones, g):
    hi, lo = _split_bf16(x * x)
    ss = (jnp.dot(hi, pair_ones, preferred_element_type=F32)
          + jnp.dot(lo, pair_ones, preferred_element_type=F32))
    return x * lax.rsqrt(ss * (1.0 / SWA_HD) + NORM_EPS) * g


def _swa_kernel(sinks_ref, q_ref, kvc_ref, kvp_ref, posc_ref, posp_ref, freq_ref, qg_ref, kg_ref,
                o_ref):
    n = pl.program_id(1)
    w = WINDOW
    lane = lax.broadcasted_iota(jnp.int32, (1, LANES), 1)
    first_half = (lane & (SWA_HD - 1)) < (SWA_HD // 2)
    low = lane < SWA_HD
    r = lax.broadcasted_iota(jnp.int32, (LANES, LANES), 0)
    c = lax.broadcasted_iota(jnp.int32, (LANES, LANES), 1)
    pair_ones = ((r < SWA_HD) == (c < SWA_HD)).astype(BF16)

    def tables(pos_ref):
        ang = pos_ref[...].astype(F32) * freq_ref[...]
        sin = jnp.sin(ang)
        return jnp.cos(ang), jnp.where(first_half, -sin, sin)

    cos_c, sin_c = tables(posc_ref)
    cos_p, sin_p = tables(posp_ref)
    cos_k = jnp.concatenate([cos_p, cos_c], axis=0)
    sin_k = jnp.concatenate([sin_p, sin_c], axis=0)

    n_chunks = TOK_W // LANES
    q = q_ref[...].astype(F32)
    qs = jnp.concatenate([q[:, i * LANES:(i + 1) * LANES] for i in range(n_chunks)], axis=0)
    qs = _head_rms(qs, pair_ones, qg_ref[...])
    qs = _rope(qs, jnp.concatenate([cos_c] * n_chunks, axis=0),
               jnp.concatenate([sin_c] * n_chunks, axis=0), first_half).astype(BF16)

    kv = jnp.concatenate([kvp_ref[...], kvc_ref[...]], axis=0).astype(F32)
    kn = [_rope(_head_rms(kv[:, i * LANES:(i + 1) * LANES], pair_ones, kg_ref[...]),
                cos_k, sin_k, first_half) for i in range(2)]

    def block_diag(x, in_low):
        sw = pltpu.roll(x, SWA_HD, axis=1)
        top = jnp.where(low, x if in_low else sw, 0.0)
        bot = jnp.where(low, 0.0, sw if in_low else x)
        return jnp.concatenate([top, bot], axis=0).astype(BF16)

    k_src = [(kn[0], True), (kn[0], False), (kn[1], True)]
    v_src = [(kv[:, LANES:2 * LANES], False), (kv[:, 2 * LANES:], True), (kv[:, 2 * LANES:], False)]

    qpos = lax.broadcasted_iota(jnp.int32, (w, 2 * w), 0)
    kpos = lax.broadcasted_iota(jnp.int32, (w, 2 * w), 1)
    dist = qpos + w - kpos
    valid = (dist >= 0) & (dist < w) & ((kpos >= w) | (n > 0))

    for h in range(SWA_KV_HEADS):
        kd = block_diag(*k_src[h])
        vd = block_diag(*v_src[h])
        qh = qs[h * 4 * w:(h + 1) * 4 * w]
        z = lax.dot_general(qh, kd, NT_DIMS, preferred_element_type=F32) * (SWA_HD ** -0.5)
        rows = []
        for ci in range(4):
            halves = []
            for p in range(2):
                sink = sinks_ref[h * SWA_GROUP + 2 * ci + p]
                zb = jnp.where(valid, z[ci * w:(ci + 1) * w, p * 2 * w:(p + 1) * 2 * w], -jnp.inf)
                m = jnp.maximum(jnp.max(zb, axis=-1, keepdims=True), sink)
                e = jnp.exp(zb - m)
                denom = jnp.sum(e, axis=-1, keepdims=True) + jnp.exp(sink - m)
                halves.append((e / denom).astype(BF16))
            rows.append(jnp.concatenate(halves, axis=1))
        p_all = jnp.concatenate(rows, axis=0)
        out = jnp.dot(p_all, vd, preferred_element_type=F32)
        for ci in range(4):
            cols = slice((h * 4 + ci) * LANES, (h * 4 + ci + 1) * LANES)
            o_ref[:, cols] = out[ci * w:(ci + 1) * w].astype(o_ref.dtype)


def _swa_attention(proj, positions, q_gain, k_gain, sinks):
    b, s, _ = proj.shape
    w = WINDOW
    kv_blk = TOK_W // (2 * SWA_KV_W)
    half = SWA_HD // 2
    inv_freq = ROPE_THETA ** (-jnp.arange(half, dtype=F32) / half)
    freq = jnp.tile(inv_freq, LANES // half).reshape(1, LANES)
    pos = positions.reshape(b, s, 1)
    grid_spec = pltpu.PrefetchScalarGridSpec(
        num_scalar_prefetch=1,
        grid=(b, s // w),
        in_specs=[
            pl.BlockSpec((None, w, TOK_W), lambda bi, i, sk: (bi, i, 0)),
            pl.BlockSpec((None, w, 2 * SWA_KV_W), lambda bi, i, sk: (bi, i, kv_blk)),
            pl.BlockSpec((None, w, 2 * SWA_KV_W), lambda bi, i, sk: (bi, jnp.maximum(i - 1, 0), kv_blk)),
            pl.BlockSpec((None, w, 1), lambda bi, i, sk: (bi, i, 0)),
            pl.BlockSpec((None, w, 1), lambda bi, i, sk: (bi, jnp.maximum(i - 1, 0), 0)),
            pl.BlockSpec((1, LANES), lambda bi, i, sk: (0, 0)),
            pl.BlockSpec((1, LANES), lambda bi, i, sk: (0, 0)),
            pl.BlockSpec((1, LANES), lambda bi, i, sk: (0, 0)),
        ],
        out_specs=pl.BlockSpec((None, w, TOK_W), lambda bi, i, sk: (bi, i, 0)),
    )
    return pl.pallas_call(
        _swa_kernel,
        out_shape=jax.ShapeDtypeStruct((b, s, TOK_W), BF16),
        grid_spec=grid_spec,
        compiler_params=_params(("parallel", "arbitrary")),
        name="swa_attention",
    )(sinks.astype(F32), proj, proj, proj, pos, pos, freq,
      jnp.tile(q_gain, 2).reshape(1, LANES), jnp.tile(k_gain, 2).reshape(1, LANES))


def _mix_out_kernel(x_ref, tok_ref, q0_ref, q1_ref, q2_ref, q3_ref, km_ref, vm_ref, qg_ref, wo_ref,
                    o_ref):
    acc = x_ref[...] + jnp.dot(tok_ref[...], wo_ref[:TOK_W, :], preferred_element_type=F32)
    memo = []
    for h, q_ref in enumerate((q0_ref, q1_ref, q2_ref, q3_ref)):
        cols = slice(h * MEM_HD, (h + 1) * MEM_HD)
        qn = _rms(q_ref[...].astype(F32), qg_ref[...]).astype(BF16)
        z = lax.dot_general(qn, km_ref[:, cols], NT_DIMS, preferred_element_type=F32) * (MEM_HD ** -0.5)
        e = jnp.exp(z - jnp.max(z, axis=-1, keepdims=True))
        p = e / jnp.sum(e, axis=-1, keepdims=True)
        memo.append(jnp.dot(p.astype(BF16), vm_ref[:, cols], preferred_element_type=F32).astype(BF16))
    memo = jnp.concatenate(memo, axis=1)
    o_ref[...] = acc + jnp.dot(memo, wo_ref[TOK_W:, :], preferred_element_type=F32)


def _mix_out(x, tok, proj, km, vm, q_gain, w_out, layer, *, memq_col, tm=512):
    b, s, d = x.shape
    qblk = memq_col // MEM_HD
    q_specs = [pl.BlockSpec((None, tm, MEM_HD), functools.partial(lambda bi, i, h: (bi, i, qblk + h), h=h))
               for h in range(MEM_HEADS)]
    return pl.pallas_call(
        _mix_out_kernel,
        out_shape=jax.ShapeDtypeStruct((b, s, d), F32),
        grid=(b, s // tm),
        in_specs=[
            pl.BlockSpec((None, tm, d), lambda bi, i: (bi, i, 0)),
            pl.BlockSpec((None, tm, TOK_W), lambda bi, i: (bi, i, 0)),
            *q_specs,
            pl.BlockSpec((None, MEM_LEN, MEM_W), lambda bi, i: (bi, 0, 0)),
            pl.BlockSpec((None, MEM_LEN, MEM_W), lambda bi, i: (bi, 0, 0)),
            pl.BlockSpec((1, MEM_HD), lambda bi, i: (0, 0)),
            pl.BlockSpec((None, d, d), lambda bi, i: (layer, 0, 0)),
        ],
        out_specs=pl.BlockSpec((None, tm, d), lambda bi, i: (bi, i, 0)),
        compiler_params=_params(("parallel", "arbitrary")),
        name="mix_out",
    )(x, tok, proj, proj, proj, proj, km, vm, q_gain.reshape(1, MEM_HD), w_out)


def kernel(x, mem, positions, ffn_norm_g, ffn_w_gate, ffn_w_up, ffn_w_down, mix_norm_g, mem_norm_g,
           w_mem_kv, memq_norm_g, memk_norm_g, w_out, sb_w_in, swa_w_in, swa_q_norm_g, swa_k_norm_g,
           swa_sinks):
    b, s, d = x.shape
    depth = ffn_norm_g.shape[0]
    wg, wu, wd = ffn_w_gate.astype(BF16), ffn_w_up.astype(BF16), ffn_w_down.astype(BF16)
    w_kv, w_o = w_mem_kv.astype(BF16), w_out.astype(BF16)
    for i in range(depth):
        x2 = _ffn(x.reshape(b * s, d), ffn_norm_g[i, 0], wg, wu, wd, i, 0)
        km, vm = _mem_kv(mem, mem_norm_g[i], w_kv, memk_norm_g[i], i)
        j = i // 2
        if i % 2 == 0:
            col_scale = jnp.where(jnp.arange(sb_w_in.shape[2]) < TOK_W, SB_HD ** -0.5, 1.0)
            w_in = (sb_w_in[j] * col_scale).astype(BF16)
            proj = _norm_proj(x2, mix_norm_g[i], w_in, tn=1024).reshape(b, s, -1)
            tok = _sb_attention(proj)
            memq_col = 3 * TOK_W
        else:
            w_in = swa_w_in[j].astype(BF16)
            proj = _norm_proj(x2, mix_norm_g[i], w_in, tn=w_in.shape[1]).reshape(b, s, -1)
            tok = _swa_attention(proj, positions, swa_q_norm_g[j], swa_k_norm_g[j], swa_sinks[j])
            memq_col = TOK_W + 2 * SWA_KV_W
        x3 = _mix_out(x2.reshape(b, s, d), tok, proj, km, vm, memq_norm_g[i], w_o, i,
                      memq_col=memq_col)
        x = _ffn(x3.reshape(b * s, d), ffn_norm_g[i, 1], wg, wu, wd, i, 1).reshape(b, s, d)
    return x
```

```python
import functools

import jax
import jax.numpy as jnp
from jax import lax
from jax.experimental import pallas as pl
from jax.experimental.pallas import tpu as pltpu

D_MODEL = 2048
D_FF = 5632
MEM_LEN = 256
MEM_HEADS = 4
MEM_HD = 128
MEM_W = MEM_HEADS * MEM_HD
TOK_W = D_MODEL - MEM_W
SB_HD = 128
SB_HEADS = TOK_W // SB_HD
SWA_HD = 64
SWA_HEADS = TOK_W // SWA_HD
SWA_GROUP = 8
SWA_KV_HEADS = SWA_HEADS // SWA_GROUP
SWA_KV_W = SWA_KV_HEADS * SWA_HD
WINDOW = 128
ROPE_THETA = 10000.0
NORM_EPS = 1e-6

LANES = 128
VMEM_LIMIT = 56 * 1024 * 1024

F32 = jnp.float32
BF16 = jnp.bfloat16
NT_DIMS = (((1,), (1,)), ((), ()))
TN_DIMS = (((0,), (0,)), ((), ()))


def _params(semantics, flags=None):
    return pltpu.CompilerParams(dimension_semantics=semantics, vmem_limit_bytes=VMEM_LIMIT,
                                flags=flags)


def _rms(x, g):
    ms = jnp.mean(x * x, axis=-1, keepdims=True)
    return x * lax.rsqrt(ms + NORM_EPS) * g


def _ffn_kernel(x_ref, g_ref, wg_ref, wu_ref, wd_ref, o_ref, hn_ref):
    @pl.when(pl.program_id(1) == 0)
    def _():
        x = x_ref[...]
        hn_ref[...] = _rms(x, g_ref[...]).astype(BF16)
        o_ref[...] = x

    h = hn_ref[...]
    gate = jnp.dot(h, wg_ref[...], preferred_element_type=F32)
    up = jnp.dot(h, wu_ref[...], preferred_element_type=F32)
    act = (gate * jax.nn.sigmoid(gate)) * up * 0.5
    o_ref[...] += jnp.dot(act.astype(BF16), wd_ref[...], preferred_element_type=F32)


def _ffn(x, g, wg, wu, wd, layer, half, *, tm=1024, tf=512):
    n, d = x.shape
    f = wg.shape[-1]
    return pl.pallas_call(
        _ffn_kernel,
        out_shape=jax.ShapeDtypeStruct((n, d), F32),
        grid=(n // tm, f // tf),
        in_specs=[
            pl.BlockSpec((tm, d), lambda i, j: (i, 0)),
            pl.BlockSpec((1, d), lambda i, j: (0, 0)),
            pl.BlockSpec((None, None, d, tf), lambda i, j: (layer, half, 0, j)),
            pl.BlockSpec((None, None, d, tf), lambda i, j: (layer, half, 0, j)),
            pl.BlockSpec((None, None, tf, d), lambda i, j: (layer, half, j, 0)),
        ],
        out_specs=pl.BlockSpec((tm, d), lambda i, j: (i, 0)),
        scratch_shapes=[pltpu.VMEM((tm, d), BF16)],
        compiler_params=_params(("parallel", "arbitrary")),
        name="ffn",
    )(x, g.reshape(1, d), wg, wu, wd)


def _norm_proj_kernel(x_ref, g_ref, w_ref, o_ref, hn_ref):
    @pl.when(pl.program_id(1) == 0)
    def _():
        hn_ref[...] = _rms(x_ref[...], g_ref[...]).astype(BF16)

    o_ref[...] = jnp.dot(hn_ref[...], w_ref[...], preferred_element_type=F32).astype(o_ref.dtype)


def _norm_proj(x, g, w, *, tm=1024, tn):
    n, d = x.shape
    wout = w.shape[1]
    return pl.pallas_call(
        _norm_proj_kernel,
        out_shape=jax.ShapeDtypeStruct((n, wout), BF16),
        grid=(n // tm, wout // tn),
        in_specs=[
            pl.BlockSpec((tm, d), lambda i, j: (i, 0)),
            pl.BlockSpec((1, d), lambda i, j: (0, 0)),
            pl.BlockSpec((d, tn), lambda i, j: (0, j)),
        ],
        out_specs=pl.BlockSpec((tm, tn), lambda i, j: (i, j)),
        scratch_shapes=[pltpu.VMEM((tm, d), BF16)],
        compiler_params=_params(("parallel", "arbitrary")),
        name="norm_proj",
    )(x, g.reshape(1, d), w)


def _mem_kv_kernel(mem_ref, g_ref, w_ref, kg_ref, km_ref, vm_ref):
    mem_n = _rms(mem_ref[...], g_ref[...]).astype(BF16)
    kv = jnp.dot(mem_n, w_ref[...], preferred_element_type=F32)
    for h in range(MEM_HEADS):
        cols = slice(h * MEM_HD, (h + 1) * MEM_HD)
        km_ref[:, cols] = _rms(kv[:, cols], kg_ref[...]).astype(BF16)
    vm_ref[...] = kv[:, MEM_W:].astype(BF16)


def _mem_kv(mem, g, w, kg, layer):
    b, m, d = mem.shape
    out = jax.ShapeDtypeStruct((b, m, MEM_W), BF16)
    return pl.pallas_call(
        _mem_kv_kernel,
        out_shape=(out, out),
        grid=(b,),
        in_specs=[
            pl.BlockSpec((None, m, d), lambda i: (i, 0, 0)),
            pl.BlockSpec((1, d), lambda i: (0, 0)),
            pl.BlockSpec((None, d, 2 * MEM_W), lambda i: (layer, 0, 0)),
            pl.BlockSpec((1, MEM_HD), lambda i: (0, 0)),
        ],
        out_specs=(pl.BlockSpec((None, m, MEM_W), lambda i: (i, 0, 0)),
                   pl.BlockSpec((None, m, MEM_W), lambda i: (i, 0, 0))),
        compiler_params=_params(("parallel",)),
        name="mem_kv",
    )(mem, g.reshape(1, d), w, kg.reshape(1, MEM_HD))


SB_CARRY_DONE = 110.0


def _split_bf16(x):
    hi = x.astype(BF16)
    lo = (x - hi.astype(F32)).astype(BF16)
    return hi, lo


def _sb_kernel(q_ref, k_ref, v_ref, o_ref, *, blk, qw, heads):
    qi = pl.program_id(2)
    diag_blocks = qw // blk
    half = blk // 2
    row = lax.broadcasted_iota(jnp.int32, (half, half), 0)
    col = lax.broadcasted_iota(jnp.int32, (half, half), 1)
    suffix_ones = (col >= row).astype(BF16)
    suffix_ones2 = jnp.concatenate([suffix_ones, suffix_ones], axis=1)
    key_idx = lax.broadcasted_iota(jnp.int32, (half, qw), 0)
    query_idx = lax.broadcasted_iota(jnp.int32, (half, qw), 1)
    hs = range(heads)
    cols = [slice(h * SB_HD, (h + 1) * SB_HD) for h in hs]
    rows = [slice(0, half), slice(half, blk)]
    qs = [q_ref[:, cols[h]] for h in hs]

    def block(j, state, key_offset):
        start = pl.multiple_of(j * blk, blk)
        masked = key_offset is not None
        if masked:
            causal = [(key_idx + (key_offset + p * half)) < query_idx for p in range(2)]
        zs = [lax.dot_general(k_ref[pl.ds(start, blk), cols[h]], qs[h], NT_DIMS,
                              preferred_element_type=F32) for h in hs]
        zs = [[z[rows[p]] for p in range(2)] for z in zs]

        def suffix_sum(z, p):
            sp = jnp.maximum(z, 0.0) + jnp.log(1.0 + jnp.exp(-jnp.abs(z)))
            if masked:
                sp = jnp.where(causal[p], sp, 0.0)
            return jnp.dot(suffix_ones2, jnp.concatenate(_split_bf16(sp), axis=0),
                           preferred_element_type=F32)

        incls = [[suffix_sum(zh[p], p) for p in range(2)] for zh in zs]
        new_state = []
        for h in hs:
            carry, acc = state[h]
            late = jnp.exp(zs[h][1] - incls[h][1] - carry)
            carry = carry + incls[h][1][0:1, :]
            early = jnp.exp(zs[h][0] - incls[h][0] - carry)
            carry = carry + incls[h][0][0:1, :]
            if masked:
                early = jnp.where(causal[0], early, 0.0)
                late = jnp.where(causal[1], late, 0.0)
            probs = jnp.concatenate([early.astype(BF16), late.astype(BF16)], axis=0)
            acc = acc + lax.dot_general(v_ref[pl.ds(start, blk), cols[h]], probs, TN_DIMS,
                                        preferred_element_type=F32)
            new_state.append((carry, acc))
        return tuple(new_state)

    state = tuple((jnp.zeros((1, qw), F32), jnp.zeros((SB_HD, qw), F32)) for _ in hs)
    first = qi * diag_blocks
    for r in reversed(range(diag_blocks)):
        state = block(first + r, state, r * blk)

    def unfinished(st):
        least = functools.reduce(jnp.minimum, [st[h][0] for h in hs])
        return (jnp.min(least) < SB_CARRY_DONE).astype(jnp.int32)

    def sweep(c):
        i, _, st = c
        st = block(first - 1 - i, st, None)
        return i + 1, unfinished(st), st

    _, _, state = lax.while_loop(lambda c: jnp.logical_and(c[0] < first, c[1] > 0), sweep,
                                 (jnp.int32(0), unfinished(state), state))
    for h in hs:
        o_ref[:, cols[h]] = state[h][1].T.astype(o_ref.dtype)


def _sb_attention(proj, *, blk=256, qw=256, heads=6):
    b, s, _ = proj.shape
    gw = heads * SB_HD
    groups = TOK_W // gw
    return pl.pallas_call(
        functools.partial(_sb_kernel, blk=blk, qw=qw, heads=heads),
        out_shape=jax.ShapeDtypeStruct((b, s, TOK_W), BF16),
        grid=(b, groups, s // qw),
        in_specs=[
            pl.BlockSpec((None, qw, gw), lambda bi, g, i: (bi, i, g)),
            pl.BlockSpec((None, s, gw), lambda bi, g, i: (bi, 0, groups + g)),
            pl.BlockSpec((None, s, gw), lambda bi, g, i: (bi, 0, 2 * groups + g)),
        ],
        out_specs=pl.BlockSpec((None, qw, gw), lambda bi, g, i: (bi, i, g)),
        compiler_params=_params(("parallel", "parallel", "arbitrary")),
        name="sb_attention",
    )(proj, proj, proj)


def _rope(x, cos, sin_signed, first_half):
    partner = jnp.where(first_half, pltpu.roll(x, 96, axis=1), pltpu.roll(x, 32, axis=1))
    return x * cos + partner * sin_signed


def _head_rms(x, pair_ones, g):
    hi, lo = _split_bf16(x * x)
    ss = (jnp.dot(hi, pair_ones, preferred_element_type=F32)
          + jnp.dot(lo, pair_ones, preferred_element_type=F32))
    return x * lax.rsqrt(ss * (1.0 / SWA_HD) + NORM_EPS) * g


def _swa_kernel(sinks_ref, q_ref, kvc_ref, kvp_ref, posc_ref, posp_ref, freq_ref, qg_ref, kg_ref,
                o_ref):
    n = pl.program_id(1)
    w = WINDOW
    lane = lax.broadcasted_iota(jnp.int32, (1, LANES), 1)
    first_half = (lane & (SWA_HD - 1)) < (SWA_HD // 2)
    low = lane < SWA_HD
    r = lax.broadcasted_iota(jnp.int32, (LANES, LANES), 0)
    c = lax.broadcasted_iota(jnp.int32, (LANES, LANES), 1)
    pair_ones = ((r < SWA_HD) == (c < SWA_HD)).astype(BF16)

    def tables(pos_ref):
        ang = pos_ref[...].astype(F32) * freq_ref[...]
        sin = jnp.sin(ang)
        return jnp.cos(ang), jnp.where(first_half, -sin, sin)

    cos_c, sin_c = tables(posc_ref)
    cos_p, sin_p = tables(posp_ref)
    cos_k = jnp.concatenate([cos_p, cos_c], axis=0)
    sin_k = jnp.concatenate([sin_p, sin_c], axis=0)

    n_chunks = TOK_W // LANES
    q = q_ref[...].astype(F32)
    qs = jnp.concatenate([q[:, i * LANES:(i + 1) * LANES] for i in range(n_chunks)], axis=0)
    qs = _head_rms(qs, pair_ones, qg_ref[...] * (SWA_HD ** -0.5))
    qs = _rope(qs, jnp.concatenate([cos_c] * n_chunks, axis=0),
               jnp.concatenate([sin_c] * n_chunks, axis=0), first_half).astype(BF16)

    kv = jnp.concatenate([kvp_ref[...], kvc_ref[...]], axis=0).astype(F32)
    kn = [_rope(_head_rms(kv[:, i * LANES:(i + 1) * LANES], pair_ones, kg_ref[...]),
                cos_k, sin_k, first_half) for i in range(2)]

    def block_diag(x, in_low):
        sw = pltpu.roll(x, SWA_HD, axis=1)
        top = jnp.where(low, x if in_low else sw, 0.0)
        bot = jnp.where(low, 0.0, sw if in_low else x)
        return jnp.concatenate([top, bot], axis=0).astype(BF16)

    k_src = [(kn[0], True), (kn[0], False), (kn[1], True)]
    v_src = [(kv[:, LANES:2 * LANES], False), (kv[:, 2 * LANES:], True), (kv[:, 2 * LANES:], False)]

    qpos = lax.broadcasted_iota(jnp.int32, (w, 2 * w), 0)
    kpos = lax.broadcasted_iota(jnp.int32, (w, 2 * w), 1)
    dist = qpos + w - kpos
    valid = (dist >= 0) & (dist < w) & ((kpos >= w) | (n > 0))

    kv_heads = range(SWA_KV_HEADS)
    zs = [lax.dot_general(qs[h * 4 * w:(h + 1) * 4 * w], block_diag(*k_src[h]), NT_DIMS,
                          preferred_element_type=F32) for h in kv_heads]

    def softmax(z, h):
        rows = []
        for ci in range(4):
            halves = []
            for p in range(2):
                sink = sinks_ref[h * SWA_GROUP + 2 * ci + p]
                zb = jnp.where(valid, z[ci * w:(ci + 1) * w, p * 2 * w:(p + 1) * 2 * w], -jnp.inf)
                m = jnp.maximum(jnp.max(zb, axis=-1, keepdims=True), sink)
                e = jnp.exp(zb - m)
                denom = jnp.sum(e, axis=-1, keepdims=True) + jnp.exp(sink - m)
                halves.append((e / denom).astype(BF16))
            rows.append(jnp.concatenate(halves, axis=1))
        return jnp.concatenate(rows, axis=0)

    ps = [softmax(zs[h], h) for h in kv_heads]
    outs = [jnp.dot(ps[h], block_diag(*v_src[h]), preferred_element_type=F32) for h in kv_heads]
    for h in kv_heads:
        for ci in range(4):
            cols = slice((h * 4 + ci) * LANES, (h * 4 + ci + 1) * LANES)
            o_ref[:, cols] = outs[h][ci * w:(ci + 1) * w].astype(o_ref.dtype)


def _swa_attention(proj, positions, q_gain, k_gain, sinks):
    b, s, _ = proj.shape
    w = WINDOW
    kv_blk = TOK_W // (2 * SWA_KV_W)
    half = SWA_HD // 2
    inv_freq = ROPE_THETA ** (-jnp.arange(half, dtype=F32) / half)
    freq = jnp.tile(inv_freq, LANES // half).reshape(1, LANES)
    pos = positions.reshape(b, s, 1)
    grid_spec = pltpu.PrefetchScalarGridSpec(
        num_scalar_prefetch=1,
        grid=(b, s // w),
        in_specs=[
            pl.BlockSpec((None, w, TOK_W), lambda bi, i, sk: (bi, i, 0)),
            pl.BlockSpec((None, w, 2 * SWA_KV_W), lambda bi, i, sk: (bi, i, kv_blk)),
            pl.BlockSpec((None, w, 2 * SWA_KV_W), lambda bi, i, sk: (bi, jnp.maximum(i - 1, 0), kv_blk)),
            pl.BlockSpec((None, w, 1), lambda bi, i, sk: (bi, i, 0)),
            pl.BlockSpec((None, w, 1), lambda bi, i, sk: (bi, jnp.maximum(i - 1, 0), 0)),
            pl.BlockSpec((1, LANES), lambda bi, i, sk: (0, 0)),
            pl.BlockSpec((1, LANES), lambda bi, i, sk: (0, 0)),
            pl.BlockSpec((1, LANES), lambda bi, i, sk: (0, 0)),
        ],
        out_specs=pl.BlockSpec((None, w, TOK_W), lambda bi, i, sk: (bi, i, 0)),
    )
    return pl.pallas_call(
        _swa_kernel,
        out_shape=jax.ShapeDtypeStruct((b, s, TOK_W), BF16),
        grid_spec=grid_spec,
        compiler_params=_params(("parallel", "arbitrary")),
        name="swa_attention",
    )(sinks.astype(F32), proj, proj, proj, pos, pos, freq,
      jnp.tile(q_gain, 2).reshape(1, LANES), jnp.tile(k_gain, 2).reshape(1, LANES))


def _mix_out_kernel(x_ref, tok_ref, q0_ref, q1_ref, q2_ref, q3_ref, km_ref, vm_ref, qg_ref, wo_ref,
                    o_ref):
    acc = x_ref[...] + jnp.dot(tok_ref[...], wo_ref[:TOK_W, :], preferred_element_type=F32)
    memo = []
    for h, q_ref in enumerate((q0_ref, q1_ref, q2_ref, q3_ref)):
        cols = slice(h * MEM_HD, (h + 1) * MEM_HD)
        qn = _rms(q_ref[...].astype(F32), qg_ref[...]).astype(BF16)
        z = lax.dot_general(qn, km_ref[:, cols], NT_DIMS, preferred_element_type=F32) * (MEM_HD ** -0.5)
        e = jnp.exp(z - jnp.max(z, axis=-1, keepdims=True))
        p = e / jnp.sum(e, axis=-1, keepdims=True)
        memo.append(jnp.dot(p.astype(BF16), vm_ref[:, cols], preferred_element_type=F32).astype(BF16))
    memo = jnp.concatenate(memo, axis=1)
    o_ref[...] = acc + jnp.dot(memo, wo_ref[TOK_W:, :], preferred_element_type=F32)


def _mix_out(x, tok, proj, km, vm, q_gain, w_out, layer, *, memq_col, tm=512):
    b, s, d = x.shape
    qblk = memq_col // MEM_HD
    q_specs = [pl.BlockSpec((None, tm, MEM_HD), functools.partial(lambda bi, i, h: (bi, i, qblk + h), h=h))
               for h in range(MEM_HEADS)]
    return pl.pallas_call(
        _mix_out_kernel,
        out_shape=jax.ShapeDtypeStruct((b, s, d), F32),
        grid=(b, s // tm),
        in_specs=[
            pl.BlockSpec((None, tm, d), lambda bi, i: (bi, i, 0)),
            pl.BlockSpec((None, tm, TOK_W), lambda bi, i: (bi, i, 0)),
            *q_specs,
            pl.BlockSpec((None, MEM_LEN, MEM_W), lambda bi, i: (bi, 0, 0)),
            pl.BlockSpec((None, MEM_LEN, MEM_W), lambda bi, i: (bi, 0, 0)),
            pl.BlockSpec((1, MEM_HD), lambda bi, i: (0, 0)),
            pl.BlockSpec((None, d, d), lambda bi, i: (layer, 0, 0)),
        ],
        out_specs=pl.BlockSpec((None, tm, d), lambda bi, i: (bi, i, 0)),
        compiler_params=_params(("parallel", "arbitrary")),
        name="mix_out",
    )(x, tok, proj, proj, proj, proj, km, vm, q_gain.reshape(1, MEM_HD), w_out)


def kernel(x, mem, positions, ffn_norm_g, ffn_w_gate, ffn_w_up, ffn_w_down, mix_norm_g, mem_norm_g,
           w_mem_kv, memq_norm_g, memk_norm_g, w_out, sb_w_in, swa_w_in, swa_q_norm_g, swa_k_norm_g,
           swa_sinks):
    b, s, d = x.shape
    depth = ffn_norm_g.shape[0]
    wg, wu, wd = ffn_w_gate.astype(BF16), ffn_w_up.astype(BF16), ffn_w_down.astype(BF16)
    w_kv, w_o = w_mem_kv.astype(BF16), w_out.astype(BF16)
    for i in range(depth):
        x2 = _ffn(x.reshape(b * s, d), ffn_norm_g[i, 0], wg, wu, wd, i, 0)
        km, vm = _mem_kv(mem, mem_norm_g[i], w_kv, memk_norm_g[i], i)
        j = i // 2
        if i % 2 == 0:
            col_scale = jnp.where(jnp.arange(sb_w_in.shape[2]) < TOK_W, SB_HD ** -0.5, 1.0)
            w_in = (sb_w_in[j] * col_scale).astype(BF16)
            proj = _norm_proj(x2, mix_norm_g[i], w_in, tn=1024).reshape(b, s, -1)
            tok = _sb_attention(proj)
            memq_col = 3 * TOK_W
        else:
            w_in = swa_w_in[j].astype(BF16)
            proj = _norm_proj(x2, mix_norm_g[i], w_in, tn=w_in.shape[1]).reshape(b, s, -1)
            tok = _swa_attention(proj, positions, swa_q_norm_g[j], swa_k_norm_g[j], swa_sinks[j])
            memq_col = TOK_W + 2 * SWA_KV_W
        x3 = _mix_out(x2.reshape(b, s, d), tok, proj, km, vm, memq_norm_g[i], w_o, i,
                      memq_col=memq_col)
        x = _ffn(x3.reshape(b * s, d), ffn_norm_g[i, 1], wg, wu, wd, i, 1).reshape(b, s, d)
    return x
```

```python
import functools

import jax
import jax.numpy as jnp
from jax import lax
from jax.experimental import pallas as pl
from jax.experimental.pallas import tpu as pltpu

D_MODEL = 2048
D_FF = 5632
MEM_LEN = 256
MEM_HEADS = 4
MEM_HD = 128
MEM_W = MEM_HEADS * MEM_HD
TOK_W = D_MODEL - MEM_W
SB_HD = 128
SB_HEADS = TOK_W // SB_HD
SWA_HD = 64
SWA_HEADS = TOK_W // SWA_HD
SWA_GROUP = 8
SWA_KV_HEADS = SWA_HEADS // SWA_GROUP
SWA_KV_W = SWA_KV_HEADS * SWA_HD
WINDOW = 128
ROPE_THETA = 10000.0
NORM_EPS = 1e-6
LOG2E = 1.4426950408889634

LANES = 128
VMEM_LIMIT = 58 * 1024 * 1024

F32 = jnp.float32
BF16 = jnp.bfloat16
NT_DIMS = (((1,), (1,)), ((), ()))
TN_DIMS = (((0,), (0,)), ((), ()))


def _params(semantics, flags=None):
    return pltpu.CompilerParams(dimension_semantics=semantics, vmem_limit_bytes=VMEM_LIMIT,
                                flags=flags)


def _rms(x, g):
    ms = jnp.mean(x * x, axis=-1, keepdims=True)
    return x * lax.rsqrt(ms + NORM_EPS) * g


def _ffn_kernel(x_ref, g_ref, wg_ref, wu_ref, wd_ref, *refs):
    n_cast = (len(refs) - 2) // 2
    cast_in, o_ref, cast_out, hn_ref = refs[:n_cast], refs[n_cast], refs[n_cast + 1:-1], refs[-1]

    @pl.when(pl.program_id(1) == 0)
    def _():
        x = x_ref[...]
        hn_ref[...] = _rms(x, g_ref[...]).astype(BF16)
        o_ref[...] = x

    h = hn_ref[...]
    gate = jnp.dot(h, wg_ref[...], preferred_element_type=F32)
    up = jnp.dot(h, wu_ref[...], preferred_element_type=F32)
    act = (gate * jax.nn.sigmoid(gate)) * up * 0.5
    o_ref[...] += jnp.dot(act.astype(BF16), wd_ref[...], preferred_element_type=F32)
    for src, dst in zip(cast_in, cast_out):
        dst[...] = src[...].astype(BF16)


def _ffn(x, g, w, w_f32=None, nxt=None, *, tm=1024, tf=512):
    n, d = x.shape
    f = w[0].shape[1]
    assert n % tm == 0 and f % tf == 0
    ni = n // tm
    in_specs = [
        pl.BlockSpec((tm, d), lambda i, j: (i, 0)),
        pl.BlockSpec((1, d), lambda i, j: (0, 0)),
        pl.BlockSpec((d, tf), lambda i, j: (0, j)),
        pl.BlockSpec((d, tf), lambda i, j: (0, j)),
        pl.BlockSpec((tf, d), lambda i, j: (j, 0)),
    ]
    out_shape = [jax.ShapeDtypeStruct((n, d), F32)]
    out_specs = [pl.BlockSpec((tm, d), lambda i, j: (i, 0))]
    cast = ()
    if nxt is not None:
        assert d % (16 * ni) == 0
        layer, half = nxt
        rows = d // ni
        cast = w_f32
        in_specs += [pl.BlockSpec((None, None, rows, tf), lambda i, j: (layer, half, i, j)),
                     pl.BlockSpec((None, None, rows, tf), lambda i, j: (layer, half, i, j)),
                     pl.BlockSpec((None, None, tf, rows), lambda i, j: (layer, half, j, i))]
        out_shape += [jax.ShapeDtypeStruct(c.shape[2:], BF16) for c in cast]
        out_specs += [pl.BlockSpec((rows, tf), lambda i, j: (i, j)),
                      pl.BlockSpec((rows, tf), lambda i, j: (i, j)),
                      pl.BlockSpec((tf, rows), lambda i, j: (j, i))]
    out = pl.pallas_call(
        _ffn_kernel,
        out_shape=out_shape,
        grid=(ni, f // tf),
        in_specs=in_specs,
        out_specs=out_specs,
        scratch_shapes=[pltpu.VMEM((tm, d), BF16)],
        compiler_params=_params(("parallel", "arbitrary")),
        name="ffn",
    )(x, g.reshape(1, d), *w, *cast)
    return out[0], tuple(out[1:])


def _norm_proj_kernel(x_ref, g_ref, w_ref, o_ref, hn_ref):
    @pl.when(pl.program_id(1) == 0)
    def _():
        hn_ref[...] = _rms(x_ref[...], g_ref[...]).astype(BF16)

    o_ref[...] = jnp.dot(hn_ref[...], w_ref[...], preferred_element_type=F32).astype(o_ref.dtype)


def _norm_proj(x, g, w, *, tm=1024, tn):
    n, d = x.shape
    wout = w.shape[1]
    return pl.pallas_call(
        _norm_proj_kernel,
        out_shape=jax.ShapeDtypeStruct((n, wout), BF16),
        grid=(n // tm, wout // tn),
        in_specs=[
            pl.BlockSpec((tm, d), lambda i, j: (i, 0)),
            pl.BlockSpec((1, d), lambda i, j: (0, 0)),
            pl.BlockSpec((d, tn), lambda i, j: (0, j)),
        ],
        out_specs=pl.BlockSpec((tm, tn), lambda i, j: (i, j)),
        scratch_shapes=[pltpu.VMEM((tm, d), BF16)],
        compiler_params=_params(("parallel", "arbitrary")),
        name="norm_proj",
    )(x, g.reshape(1, d), w)


def _mem_kv_kernel(mem_ref, g_ref, w_ref, kg_ref, km_ref, vm_ref):
    mem_n = _rms(mem_ref[...], g_ref[...]).astype(BF16)
    kv = jnp.dot(mem_n, w_ref[...], preferred_element_type=F32)
    for h in range(MEM_HEADS):
        cols = slice(h * MEM_HD, (h + 1) * MEM_HD)
        km_ref[:, cols] = _rms(kv[:, cols], kg_ref[...]).astype(BF16)
    vm_ref[...] = kv[:, MEM_W:].astype(BF16)


def _mem_kv(mem, g, w, kg, layer):
    b, m, d = mem.shape
    out = jax.ShapeDtypeStruct((b, m, MEM_W), BF16)
    return pl.pallas_call(
        _mem_kv_kernel,
        out_shape=(out, out),
        grid=(b,),
        in_specs=[
            pl.BlockSpec((None, m, d), lambda i: (i, 0, 0)),
            pl.BlockSpec((1, d), lambda i: (0, 0)),
            pl.BlockSpec((None, d, 2 * MEM_W), lambda i: (layer, 0, 0)),
            pl.BlockSpec((1, MEM_HD), lambda i: (0, 0)),
        ],
        out_specs=(pl.BlockSpec((None, m, MEM_W), lambda i: (i, 0, 0)),
                   pl.BlockSpec((None, m, MEM_W), lambda i: (i, 0, 0))),
        compiler_params=_params(("parallel",)),
        name="mem_kv",
    )(mem, g.reshape(1, d), w, kg.reshape(1, MEM_HD))


SB_CARRY_DONE = 110.0


def _split_bf16(x):
    hi = x.astype(BF16)
    lo = (x - hi.astype(F32)).astype(BF16)
    return hi, lo


def _sb_kernel(q_ref, k_ref, v_ref, o_ref, *, blk, qw, heads):
    qi = pl.program_id(2)
    diag_blocks = qw // blk
    half = blk // 2
    row = lax.broadcasted_iota(jnp.int32, (half, half), 0)
    col = lax.broadcasted_iota(jnp.int32, (half, half), 1)
    suffix_ones = (col >= row).astype(BF16)
    suffix_ones2 = jnp.concatenate([suffix_ones, suffix_ones], axis=1)
    key_idx = lax.broadcasted_iota(jnp.int32, (half, qw), 0)
    query_idx = lax.broadcasted_iota(jnp.int32, (half, qw), 1)
    hs = range(heads)
    cols = [slice(h * SB_HD, (h + 1) * SB_HD) for h in hs]
    rows = [slice(0, half), slice(half, blk)]
    qs = [q_ref[:, cols[h]] for h in hs]

    def block(j, state, key_offset):
        start = pl.multiple_of(j * blk, blk)
        masked = key_offset is not None
        if masked:
            causal = [(key_idx + (key_offset + p * half)) < query_idx for p in range(2)]
        zs = [lax.dot_general(k_ref[pl.ds(start, blk), cols[h]], qs[h], NT_DIMS,
                              preferred_element_type=F32) for h in hs]
        zs = [[z[rows[p]] for p in range(2)] for z in zs]

        def suffix_sum(z, p):
            sp = jnp.maximum(z, 0.0) + jnp.log(1.0 + jnp.exp(-jnp.abs(z)))
            if masked:
                sp = jnp.where(causal[p], sp, 0.0)
            return jnp.dot(suffix_ones2, jnp.concatenate(_split_bf16(sp), axis=0),
                           preferred_element_type=F32)

        incls = [[suffix_sum(zh[p], p) for p in range(2)] for zh in zs]
        new_state = []
        for h in hs:
            carry, acc = state[h]
            late = jnp.exp(zs[h][1] - incls[h][1] - carry)
            carry = carry + incls[h][1][0:1, :]
            early = jnp.exp(zs[h][0] - incls[h][0] - carry)
            carry = carry + incls[h][0][0:1, :]
            if masked:
                early = jnp.where(causal[0], early, 0.0)
                late = jnp.where(causal[1], late, 0.0)
            probs = jnp.concatenate([early.astype(BF16), late.astype(BF16)], axis=0)
            acc = acc + lax.dot_general(v_ref[pl.ds(start, blk), cols[h]], probs, TN_DIMS,
                                        preferred_element_type=F32)
            new_state.append((carry, acc))
        return tuple(new_state)

    state = tuple((jnp.zeros((1, qw), F32), jnp.zeros((SB_HD, qw), F32)) for _ in hs)
    first = qi * diag_blocks
    for r in reversed(range(diag_blocks)):
        state = block(first + r, state, r * blk)

    def unfinished(st):
        least = functools.reduce(jnp.minimum, [st[h][0] for h in hs])
        return (jnp.min(least) < SB_CARRY_DONE).astype(jnp.int32)

    def sweep(c):
        i, _, st = c
        st = block(first - 1 - i, st, None)
        return i + 1, unfinished(st), st

    _, _, state = lax.while_loop(lambda c: jnp.logical_and(c[0] < first, c[1] > 0), sweep,
                                 (jnp.int32(0), unfinished(state), state))
    for h in hs:
        o_ref[:, cols[h]] = state[h][1].T.astype(o_ref.dtype)


def _sb_attention(proj, *, blk=256, qw=256, heads=6):
    b, s, _ = proj.shape
    gw = heads * SB_HD
    groups = TOK_W // gw
    return pl.pallas_call(
        functools.partial(_sb_kernel, blk=blk, qw=qw, heads=heads),
        out_shape=jax.ShapeDtypeStruct((b, s, TOK_W), BF16),
        grid=(b, groups, s // qw),
        in_specs=[
            pl.BlockSpec((None, qw, gw), lambda bi, g, i: (bi, i, g)),
            pl.BlockSpec((None, s, gw), lambda bi, g, i: (bi, 0, groups + g)),
            pl.BlockSpec((None, s, gw), lambda bi, g, i: (bi, 0, 2 * groups + g)),
        ],
        out_specs=pl.BlockSpec((None, qw, gw), lambda bi, g, i: (bi, i, g)),
        compiler_params=_params(("parallel", "parallel", "arbitrary")),
        name="sb_attention",
    )(proj, proj, proj)


def _rope(x, cos, sin_signed, first_half):
    partner = jnp.where(first_half, pltpu.roll(x, 96, axis=1), pltpu.roll(x, 32, axis=1))
    return x * cos + partner * sin_signed


def _head_rms(x, pair_ones, g):
    hi, lo = _split_bf16(x * x)
    ss = (jnp.dot(hi, pair_ones, preferred_element_type=F32)
          + jnp.dot(lo, pair_ones, preferred_element_type=F32))
    return x * lax.rsqrt(ss * (1.0 / SWA_HD) + NORM_EPS) * g


def _swa_kernel(sinks_ref, q_ref, kvc_ref, kvp_ref, posc_ref, posp_ref, freq_ref, qg_ref, kg_ref,
                o_ref):
    n = pl.program_id(1)
    w = WINDOW
    lane = lax.broadcasted_iota(jnp.int32, (1, LANES), 1)
    first_half = (lane & (SWA_HD - 1)) < (SWA_HD // 2)
    low = lane < SWA_HD
    r = lax.broadcasted_iota(jnp.int32, (LANES, LANES), 0)
    c = lax.broadcasted_iota(jnp.int32, (LANES, LANES), 1)
    pair_ones = ((r < SWA_HD) == (c < SWA_HD)).astype(BF16)

    def tables(pos_ref):
        ang = pos_ref[...].astype(F32) * freq_ref[...]
        sin = jnp.sin(ang)
        return jnp.cos(ang), jnp.where(first_half, -sin, sin)

    cos_c, sin_c = tables(posc_ref)
    cos_p, sin_p = tables(posp_ref)
    cos_k = jnp.concatenate([cos_p, cos_c], axis=0)
    sin_k = jnp.concatenate([sin_p, sin_c], axis=0)

    n_chunks = TOK_W // LANES
    q = q_ref[...].astype(F32)
    qs = jnp.concatenate([q[:, i * LANES:(i + 1) * LANES] for i in range(n_chunks)], axis=0)
    qs = _head_rms(qs, pair_ones, qg_ref[...] * (LOG2E * SWA_HD ** -0.5))
    qs = _rope(qs, jnp.concatenate([cos_c] * n_chunks, axis=0),
               jnp.concatenate([sin_c] * n_chunks, axis=0), first_half).astype(BF16)

    kv = jnp.concatenate([kvp_ref[...], kvc_ref[...]], axis=0).astype(F32)
    kn = [_rope(_head_rms(kv[:, i * LANES:(i + 1) * LANES], pair_ones, kg_ref[...]),
                cos_k, sin_k, first_half) for i in range(2)]

    def block_diag(x, in_low):
        sw = pltpu.roll(x, SWA_HD, axis=1)
        top = jnp.where(low, x if in_low else sw, 0.0)
        bot = jnp.where(low, 0.0, sw if in_low else x)
        return jnp.concatenate([top, bot], axis=0).astype(BF16)

    k_src = [(kn[0], True), (kn[0], False), (kn[1], True)]
    v_src = [(kv[:, LANES:2 * LANES], False), (kv[:, 2 * LANES:], True), (kv[:, 2 * LANES:], False)]

    qpos = lax.broadcasted_iota(jnp.int32, (w, 2 * w), 0)
    kpos = lax.broadcasted_iota(jnp.int32, (w, 2 * w), 1)
    dist = qpos + w - kpos
    valid = (dist >= 0) & (dist < w) & ((kpos >= w) | (n > 0))

    kv_heads = range(SWA_KV_HEADS)
    zs = [lax.dot_general(qs[h * 4 * w:(h + 1) * 4 * w], block_diag(*k_src[h]), NT_DIMS,
                          preferred_element_type=F32) for h in kv_heads]

    def softmax(z, h):
        rows, scales = [], []
        for ci in range(4):
            halves, recips = [], []
            for p in range(2):
                sink = sinks_ref[h * SWA_GROUP + 2 * ci + p] * LOG2E
                zb = jnp.where(valid, z[ci * w:(ci + 1) * w, p * 2 * w:(p + 1) * 2 * w], -jnp.inf)
                m = jnp.maximum(jnp.max(zb, axis=-1, keepdims=True), sink)
                e = jnp.exp2(zb - m)
                recips.append(1.0 / (jnp.sum(e, axis=-1, keepdims=True) + jnp.exp2(sink - m)))
                halves.append(e.astype(BF16))
            rows.append(jnp.concatenate(halves, axis=1))
            scales.append(jnp.where(low, recips[0], recips[1]))
        return jnp.concatenate(rows, axis=0), scales

    ps = [softmax(zs[h], h) for h in kv_heads]
    outs = [jnp.dot(ps[h][0], block_diag(*v_src[h]), preferred_element_type=F32) for h in kv_heads]
    for h in kv_heads:
        for ci in range(4):
            cols = slice((h * 4 + ci) * LANES, (h * 4 + ci + 1) * LANES)
            o_ref[:, cols] = (outs[h][ci * w:(ci + 1) * w] * ps[h][1][ci]).astype(o_ref.dtype)


def _swa_attention(proj, positions, q_gain, k_gain, sinks):
    b, s, _ = proj.shape
    w = WINDOW
    kv_blk = TOK_W // (2 * SWA_KV_W)
    half = SWA_HD // 2
    inv_freq = ROPE_THETA ** (-jnp.arange(half, dtype=F32) / half)
    freq = jnp.tile(inv_freq, LANES // half).reshape(1, LANES)
    pos = positions.reshape(b, s, 1)
    grid_spec = pltpu.PrefetchScalarGridSpec(
        num_scalar_prefetch=1,
        grid=(b, s // w),
        in_specs=[
            pl.BlockSpec((None, w, TOK_W), lambda bi, i, sk: (bi, i, 0)),
            pl.BlockSpec((None, w, 2 * SWA_KV_W), lambda bi, i, sk: (bi, i, kv_blk)),
            pl.BlockSpec((None, w, 2 * SWA_KV_W), lambda bi, i, sk: (bi, jnp.maximum(i - 1, 0), kv_blk)),
            pl.BlockSpec((None, w, 1), lambda bi, i, sk: (bi, i, 0)),
            pl.BlockSpec((None, w, 1), lambda bi, i, sk: (bi, jnp.maximum(i - 1, 0), 0)),
            pl.BlockSpec((1, LANES), lambda bi, i, sk: (0, 0)),
            pl.BlockSpec((1, LANES), lambda bi, i, sk: (0, 0)),
            pl.BlockSpec((1, LANES), lambda bi, i, sk: (0, 0)),
        ],
        out_specs=pl.BlockSpec((None, w, TOK_W), lambda bi, i, sk: (bi, i, 0)),
    )
    return pl.pallas_call(
        _swa_kernel,
        out_shape=jax.ShapeDtypeStruct((b, s, TOK_W), BF16),
        grid_spec=grid_spec,
        compiler_params=_params(("parallel", "arbitrary")),
        name="swa_attention",
    )(sinks.astype(F32), proj, proj, proj, pos, pos, freq,
      jnp.tile(q_gain, 2).reshape(1, LANES), jnp.tile(k_gain, 2).reshape(1, LANES))


def _mix_out_kernel(x_ref, tok_ref, q0_ref, q1_ref, q2_ref, q3_ref, km_ref, vm_ref, qg_ref, wo_ref,
                    o_ref):
    acc = x_ref[...] + jnp.dot(tok_ref[...], wo_ref[:TOK_W, :], preferred_element_type=F32)
    memo = []
    for h, q_ref in enumerate((q0_ref, q1_ref, q2_ref, q3_ref)):
        cols = slice(h * MEM_HD, (h + 1) * MEM_HD)
        qn = _rms(q_ref[...].astype(F32), qg_ref[...]).astype(BF16)
        z = lax.dot_general(qn, km_ref[:, cols], NT_DIMS, preferred_element_type=F32) * (MEM_HD ** -0.5)
        e = jnp.exp(z - jnp.max(z, axis=-1, keepdims=True))
        p = e / jnp.sum(e, axis=-1, keepdims=True)
        memo.append(jnp.dot(p.astype(BF16), vm_ref[:, cols], preferred_element_type=F32).astype(BF16))
    memo = jnp.concatenate(memo, axis=1)
    o_ref[...] = acc + jnp.dot(memo, wo_ref[TOK_W:, :], preferred_element_type=F32)


def _mix_out(x, tok, proj, km, vm, q_gain, w_out, layer, *, memq_col, tm=512):
    b, s, d = x.shape
    qblk = memq_col // MEM_HD
    q_specs = [pl.BlockSpec((None, tm, MEM_HD), functools.partial(lambda bi, i, h: (bi, i, qblk + h), h=h))
               for h in range(MEM_HEADS)]
    return pl.pallas_call(
        _mix_out_kernel,
        out_shape=jax.ShapeDtypeStruct((b, s, d), F32),
        grid=(b, s // tm),
        in_specs=[
            pl.BlockSpec((None, tm, d), lambda bi, i: (bi, i, 0)),
            pl.BlockSpec((None, tm, TOK_W), lambda bi, i: (bi, i, 0)),
            *q_specs,
            pl.BlockSpec((None, MEM_LEN, MEM_W), lambda bi, i: (bi, 0, 0)),
            pl.BlockSpec((None, MEM_LEN, MEM_W), lambda bi, i: (bi, 0, 0)),
            pl.BlockSpec((1, MEM_HD), lambda bi, i: (0, 0)),
            pl.BlockSpec((None, d, d), lambda bi, i: (layer, 0, 0)),
        ],
        out_specs=pl.BlockSpec((None, tm, d), lambda bi, i: (bi, i, 0)),
        compiler_params=_params(("parallel", "arbitrary")),
        name="mix_out",
    )(x, tok, proj, proj, proj, proj, km, vm, q_gain.reshape(1, MEM_HD), w_out)


def kernel(x, mem, positions, ffn_norm_g, ffn_w_gate, ffn_w_up, ffn_w_down, mix_norm_g, mem_norm_g,
           w_mem_kv, memq_norm_g, memk_norm_g, w_out, sb_w_in, swa_w_in, swa_q_norm_g, swa_k_norm_g,
           swa_sinks):
    b, s, d = x.shape
    depth = ffn_norm_g.shape[0]
    w_f32 = (ffn_w_gate, ffn_w_up, ffn_w_down)
    w_ffn = tuple(w[0, 0].astype(BF16) for w in w_f32)
    w_kv, w_o = w_mem_kv.astype(BF16), w_out.astype(BF16)
    for i in range(depth):
        x2, w_ffn = _ffn(x.reshape(b * s, d), ffn_norm_g[i, 0], w_ffn, w_f32, (i, 1))
        km, vm = _mem_kv(mem, mem_norm_g[i], w_kv, memk_norm_g[i], i)
        j = i // 2
        if i % 2 == 0:
            col_scale = jnp.where(jnp.arange(sb_w_in.shape[2]) < TOK_W, SB_HD ** -0.5, 1.0)
            w_in = (sb_w_in[j] * col_scale).astype(BF16)
            proj = _norm_proj(x2, mix_norm_g[i], w_in, tn=1024).reshape(b, s, -1)
            tok = _sb_attention(proj)
            memq_col = 3 * TOK_W
        else:
            w_in = swa_w_in[j].astype(BF16)
            proj = _norm_proj(x2, mix_norm_g[i], w_in, tn=w_in.shape[1]).reshape(b, s, -1)
            tok = _swa_attention(proj, positions, swa_q_norm_g[j], swa_k_norm_g[j], swa_sinks[j])
            memq_col = TOK_W + 2 * SWA_KV_W
        x3 = _mix_out(x2.reshape(b, s, d), tok, proj, km, vm, memq_norm_g[i], w_o, i,
                      memq_col=memq_col)
        x, w_ffn = _ffn(x3.reshape(b * s, d), ffn_norm_g[i, 1], w_ffn, w_f32,
                        (i + 1, 0) if i + 1 < depth else None)
        x = x.reshape(b, s, d)
    return x
```

```python
import functools

import jax
import jax.numpy as jnp
from jax import lax
from jax.experimental import pallas as pl
from jax.experimental.pallas import tpu as pltpu

D_MODEL = 2048
D_FF = 5632
MEM_LEN = 256
MEM_HEADS = 4
MEM_HD = 128
MEM_W = MEM_HEADS * MEM_HD
TOK_W = D_MODEL - MEM_W
SB_HD = 128
SB_HEADS = TOK_W // SB_HD
SWA_HD = 64
SWA_HEADS = TOK_W // SWA_HD
SWA_GROUP = 8
SWA_KV_HEADS = SWA_HEADS // SWA_GROUP
SWA_KV_W = SWA_KV_HEADS * SWA_HD
WINDOW = 128
ROPE_THETA = 10000.0
NORM_EPS = 1e-6
LOG2E = 1.4426950408889634

LANES = 128
VMEM_LIMIT = 58 * 1024 * 1024

F32 = jnp.float32
BF16 = jnp.bfloat16
NT_DIMS = (((1,), (1,)), ((), ()))
TN_DIMS = (((0,), (0,)), ((), ()))


def _params(semantics, flags=None):
    return pltpu.CompilerParams(dimension_semantics=semantics, vmem_limit_bytes=VMEM_LIMIT,
                                flags=flags)


def _rms(x, g):
    ms = jnp.mean(x * x, axis=-1, keepdims=True)
    return x * lax.rsqrt(ms + NORM_EPS) * g


def _ffn_kernel(x_ref, g_ref, wg_ref, wu_ref, wd_ref, *refs):
    n_cast = (len(refs) - 2) // 2
    cast_in, o_ref, cast_out, hn_ref = refs[:n_cast], refs[n_cast], refs[n_cast + 1:-1], refs[-1]

    @pl.when(pl.program_id(1) == 0)
    def _():
        x = x_ref[...]
        hn_ref[...] = _rms(x, g_ref[...]).astype(BF16)
        o_ref[...] = x

    h = hn_ref[...]
    gate = jnp.dot(h, wg_ref[...], preferred_element_type=F32)
    up = jnp.dot(h, wu_ref[...], preferred_element_type=F32)
    act = (gate * jax.nn.sigmoid(gate)) * up * 0.5
    o_ref[...] += jnp.dot(act.astype(BF16), wd_ref[...], preferred_element_type=F32)
    for src, dst in zip(cast_in, cast_out):
        dst[...] = src[...].astype(BF16)


def _ffn(x, g, w, w_f32=None, nxt=None, *, tm=1024, tf=512):
    n, d = x.shape
    f = w[0].shape[1]
    assert n % tm == 0 and f % tf == 0
    ni = n // tm
    in_specs = [
        pl.BlockSpec((tm, d), lambda i, j: (i, 0)),
        pl.BlockSpec((1, d), lambda i, j: (0, 0)),
        pl.BlockSpec((d, tf), lambda i, j: (0, j)),
        pl.BlockSpec((d, tf), lambda i, j: (0, j)),
        pl.BlockSpec((tf, d), lambda i, j: (j, 0)),
    ]
    out_shape = [jax.ShapeDtypeStruct((n, d), F32)]
    out_specs = [pl.BlockSpec((tm, d), lambda i, j: (i, 0))]
    cast = ()
    if nxt is not None:
        assert d % (16 * ni) == 0
        layer, half = nxt
        rows = d // ni
        cast = w_f32
        in_specs += [pl.BlockSpec((None, None, rows, tf), lambda i, j: (layer, half, i, j)),
                     pl.BlockSpec((None, None, rows, tf), lambda i, j: (layer, half, i, j)),
                     pl.BlockSpec((None, None, tf, rows), lambda i, j: (layer, half, j, i))]
        out_shape += [jax.ShapeDtypeStruct(c.shape[2:], BF16) for c in cast]
        out_specs += [pl.BlockSpec((rows, tf), lambda i, j: (i, j)),
                      pl.BlockSpec((rows, tf), lambda i, j: (i, j)),
                      pl.BlockSpec((tf, rows), lambda i, j: (j, i))]
    out = pl.pallas_call(
        _ffn_kernel,
        out_shape=out_shape,
        grid=(ni, f // tf),
        in_specs=in_specs,
        out_specs=out_specs,
        scratch_shapes=[pltpu.VMEM((tm, d), BF16)],
        compiler_params=_params(("parallel", "arbitrary")),
        name="ffn",
    )(x, g.reshape(1, d), *w, *cast)
    return out[0], tuple(out[1:])


def _norm_proj_kernel(x_ref, g_ref, w_ref, o_ref, hn_ref):
    @pl.when(pl.program_id(1) == 0)
    def _():
        hn_ref[...] = _rms(x_ref[...], g_ref[...]).astype(BF16)

    o_ref[...] = jnp.dot(hn_ref[...], w_ref[...], preferred_element_type=F32).astype(o_ref.dtype)


def _norm_proj(x, g, w, *, tm=1024, tn):
    n, d = x.shape
    wout = w.shape[1]
    assert n % tm == 0 and wout % tn == 0
    return pl.pallas_call(
        _norm_proj_kernel,
        out_shape=jax.ShapeDtypeStruct((n, wout), BF16),
        grid=(n // tm, wout // tn),
        in_specs=[
            pl.BlockSpec((tm, d), lambda i, j: (i, 0)),
            pl.BlockSpec((1, d), lambda i, j: (0, 0)),
            pl.BlockSpec((d, tn), lambda i, j: (0, j)),
        ],
        out_specs=pl.BlockSpec((tm, tn), lambda i, j: (i, j)),
        scratch_shapes=[pltpu.VMEM((tm, d), BF16)],
        compiler_params=_params(("parallel", "arbitrary")),
        name="norm_proj",
    )(x, g.reshape(1, d), w)


def _mem_kv_kernel(mem_ref, g_ref, w_ref, kg_ref, km_ref, vm_ref):
    mem_n = _rms(mem_ref[...], g_ref[...]).astype(BF16)
    kv = jnp.dot(mem_n, w_ref[...], preferred_element_type=F32)
    for h in range(MEM_HEADS):
        cols = slice(h * MEM_HD, (h + 1) * MEM_HD)
        km_ref[:, cols] = _rms(kv[:, cols], kg_ref[...]).astype(BF16)
    vm_ref[...] = kv[:, MEM_W:].astype(BF16)


def _mem_kv(mem, g, w, kg, layer):
    b, m, d = mem.shape
    out = jax.ShapeDtypeStruct((b, m, MEM_W), BF16)
    return pl.pallas_call(
        _mem_kv_kernel,
        out_shape=(out, out),
        grid=(b,),
        in_specs=[
            pl.BlockSpec((None, m, d), lambda i: (i, 0, 0)),
            pl.BlockSpec((1, d), lambda i: (0, 0)),
            pl.BlockSpec((None, d, 2 * MEM_W), lambda i: (layer, 0, 0)),
            pl.BlockSpec((1, MEM_HD), lambda i: (0, 0)),
        ],
        out_specs=(pl.BlockSpec((None, m, MEM_W), lambda i: (i, 0, 0)),
                   pl.BlockSpec((None, m, MEM_W), lambda i: (i, 0, 0))),
        compiler_params=_params(("parallel",)),
        name="mem_kv",
    )(mem, g.reshape(1, d), w, kg.reshape(1, MEM_HD))


SB_CARRY_DONE = 110.0


def _split_bf16(x):
    hi = x.astype(BF16)
    lo = (x - hi.astype(F32)).astype(BF16)
    return hi, lo


def _sb_kernel(q_ref, k_ref, v_ref, o_ref, *, blk, qw, heads):
    qi = pl.program_id(2)
    diag_blocks = qw // blk
    half = blk // 2
    row = lax.broadcasted_iota(jnp.int32, (half, half), 0)
    col = lax.broadcasted_iota(jnp.int32, (half, half), 1)
    suffix_ones = (col >= row).astype(BF16)
    suffix_ones2 = jnp.concatenate([suffix_ones, suffix_ones], axis=1)
    key_idx = lax.broadcasted_iota(jnp.int32, (half, qw), 0)
    query_idx = lax.broadcasted_iota(jnp.int32, (half, qw), 1)
    hs = range(heads)
    cols = [slice(h * SB_HD, (h + 1) * SB_HD) for h in hs]
    rows = [slice(0, half), slice(half, blk)]
    qs = [q_ref[:, cols[h]] for h in hs]

    def block(j, state, key_offset):
        start = pl.multiple_of(j * blk, blk)
        masked = key_offset is not None
        if masked:
            causal = [(key_idx + (key_offset + p * half)) < query_idx for p in range(2)]
        zs = [lax.dot_general(k_ref[pl.ds(start, blk), cols[h]], qs[h], NT_DIMS,
                              preferred_element_type=F32) for h in hs]
        zs = [[z[rows[p]] for p in range(2)] for z in zs]

        def suffix_sum(z, p):
            sp = jnp.maximum(z, 0.0) + jnp.log(1.0 + jnp.exp(-jnp.abs(z)))
            if masked:
                sp = jnp.where(causal[p], sp, 0.0)
            return jnp.dot(suffix_ones2, jnp.concatenate(_split_bf16(sp), axis=0),
                           preferred_element_type=F32)

        incls = [[suffix_sum(zh[p], p) for p in range(2)] for zh in zs]
        new_state = []
        for h in hs:
            carry, acc = state[h]
            late = jnp.exp(zs[h][1] - incls[h][1] - carry)
            carry = carry + incls[h][1][0:1, :]
            early = jnp.exp(zs[h][0] - incls[h][0] - carry)
            carry = carry + incls[h][0][0:1, :]
            if masked:
                early = jnp.where(causal[0], early, 0.0)
                late = jnp.where(causal[1], late, 0.0)
            probs = jnp.concatenate([early.astype(BF16), late.astype(BF16)], axis=0)
            acc = acc + lax.dot_general(v_ref[pl.ds(start, blk), cols[h]], probs, TN_DIMS,
                                        preferred_element_type=F32)
            new_state.append((carry, acc))
        return tuple(new_state)

    state = tuple((jnp.zeros((1, qw), F32), jnp.zeros((SB_HD, qw), F32)) for _ in hs)
    first = qi * diag_blocks
    for r in reversed(range(diag_blocks)):
        state = block(first + r, state, r * blk)

    def unfinished(st):
        least = functools.reduce(jnp.minimum, [st[h][0] for h in hs])
        return (jnp.min(least) < SB_CARRY_DONE).astype(jnp.int32)

    def sweep(c):
        i, _, st = c
        st = block(first - 1 - i, st, None)
        return i + 1, unfinished(st), st

    _, _, state = lax.while_loop(lambda c: jnp.logical_and(c[0] < first, c[1] > 0), sweep,
                                 (jnp.int32(0), unfinished(state), state))
    for h in hs:
        o_ref[:, cols[h]] = state[h][1].T.astype(o_ref.dtype)


def _sb_attention(proj, *, blk=256, qw=256, heads=6):
    b, s, _ = proj.shape
    gw = heads * SB_HD
    groups = TOK_W // gw
    assert TOK_W % gw == 0 and s % qw == 0 and qw % blk == 0
    return pl.pallas_call(
        functools.partial(_sb_kernel, blk=blk, qw=qw, heads=heads),
        out_shape=jax.ShapeDtypeStruct((b, s, TOK_W), BF16),
        grid=(b, groups, s // qw),
        in_specs=[
            pl.BlockSpec((None, qw, gw), lambda bi, g, i: (bi, i, g)),
            pl.BlockSpec((None, s, gw), lambda bi, g, i: (bi, 0, groups + g)),
            pl.BlockSpec((None, s, gw), lambda bi, g, i: (bi, 0, 2 * groups + g)),
        ],
        out_specs=pl.BlockSpec((None, qw, gw), lambda bi, g, i: (bi, i, g)),
        compiler_params=_params(("parallel", "parallel", "arbitrary")),
        name="sb_attention",
    )(proj, proj, proj)


def _rope(x, cos, sin_signed, first_half):
    partner = jnp.where(first_half, pltpu.roll(x, 96, axis=1), pltpu.roll(x, 32, axis=1))
    return x * cos + partner * sin_signed


def _head_rms(x, pair_ones, g):
    hi, lo = _split_bf16(x * x)
    ss = (jnp.dot(hi, pair_ones, preferred_element_type=F32)
          + jnp.dot(lo, pair_ones, preferred_element_type=F32))
    return x * lax.rsqrt(ss * (1.0 / SWA_HD) + NORM_EPS) * g


def _swa_kernel(sinks_ref, q_ref, kvc_ref, kvp_ref, posc_ref, posp_ref, freq_ref, qg_ref, kg_ref,
                o_ref, cosp_ref, sinp_ref):
    n = pl.program_id(1)
    w = WINDOW
    lane = lax.broadcasted_iota(jnp.int32, (1, LANES), 1)
    first_half = (lane & (SWA_HD - 1)) < (SWA_HD // 2)
    low = lane < SWA_HD
    r = lax.broadcasted_iota(jnp.int32, (LANES, LANES), 0)
    c = lax.broadcasted_iota(jnp.int32, (LANES, LANES), 1)
    pair_ones = ((r < SWA_HD) == (c < SWA_HD)).astype(BF16)

    def tables(pos_ref):
        ang = pos_ref[...].astype(F32) * freq_ref[...]
        sin = jnp.sin(ang)
        return jnp.cos(ang), jnp.where(first_half, -sin, sin)

    cos_c, sin_c = tables(posc_ref)

    @pl.when(n == 0)
    def _():
        cos_p0, sin_p0 = tables(posp_ref)
        cosp_ref[...] = cos_p0
        sinp_ref[...] = sin_p0

    cos_p, sin_p = cosp_ref[...], sinp_ref[...]
    cosp_ref[...] = cos_c
    sinp_ref[...] = sin_c
    cos_k = jnp.concatenate([cos_p, cos_c], axis=0)
    sin_k = jnp.concatenate([sin_p, sin_c], axis=0)

    n_chunks = TOK_W // LANES
    q = q_ref[...].astype(F32)
    qs = jnp.concatenate([q[:, i * LANES:(i + 1) * LANES] for i in range(n_chunks)], axis=0)
    qs = _head_rms(qs, pair_ones, qg_ref[...] * (LOG2E * SWA_HD ** -0.5))
    qs = _rope(qs, jnp.concatenate([cos_c] * n_chunks, axis=0),
               jnp.concatenate([sin_c] * n_chunks, axis=0), first_half).astype(BF16)

    kv = jnp.concatenate([kvp_ref[...], kvc_ref[...]], axis=0).astype(F32)
    kn = [_rope(_head_rms(kv[:, i * LANES:(i + 1) * LANES], pair_ones, kg_ref[...]),
                cos_k, sin_k, first_half) for i in range(2)]

    def block_diag(x, in_low):
        sw = pltpu.roll(x, SWA_HD, axis=1)
        top = jnp.where(low, x if in_low else sw, 0.0)
        bot = jnp.where(low, 0.0, sw if in_low else x)
        return jnp.concatenate([top, bot], axis=0).astype(BF16)

    k_src = [(kn[0], True), (kn[0], False), (kn[1], True)]
    v_src = [(kv[:, LANES:2 * LANES], False), (kv[:, 2 * LANES:], True), (kv[:, 2 * LANES:], False)]

    qpos = lax.broadcasted_iota(jnp.int32, (w, 2 * w), 0)
    kpos = lax.broadcasted_iota(jnp.int32, (w, 2 * w), 1)
    dist = qpos + w - kpos
    valid = (dist >= 0) & (dist < w) & ((kpos >= w) | (n > 0))

    kv_heads = range(SWA_KV_HEADS)
    zs = [lax.dot_general(qs[h * 4 * w:(h + 1) * 4 * w], block_diag(*k_src[h]), NT_DIMS,
                          preferred_element_type=F32) for h in kv_heads]

    def softmax(z, h):
        rows, scales = [], []
        for ci in range(4):
            halves, recips = [], []
            for p in range(2):
                sink = sinks_ref[h * SWA_GROUP + 2 * ci + p] * LOG2E
                zb = jnp.where(valid, z[ci * w:(ci + 1) * w, p * 2 * w:(p + 1) * 2 * w], -jnp.inf)
                m = jnp.maximum(jnp.max(zb, axis=-1, keepdims=True), sink)
                e = jnp.exp2(zb - m)
                recips.append(1.0 / (jnp.sum(e, axis=-1, keepdims=True) + jnp.exp2(sink - m)))
                halves.append(e.astype(BF16))
            rows.append(jnp.concatenate(halves, axis=1))
            scales.append(jnp.where(low, recips[0], recips[1]))
        return jnp.concatenate(rows, axis=0), scales

    ps = [softmax(zs[h], h) for h in kv_heads]
    outs = [jnp.dot(ps[h][0], block_diag(*v_src[h]), preferred_element_type=F32) for h in kv_heads]
    for h in kv_heads:
        for ci in range(4):
            cols = slice((h * 4 + ci) * LANES, (h * 4 + ci + 1) * LANES)
            o_ref[:, cols] = (outs[h][ci * w:(ci + 1) * w] * ps[h][1][ci]).astype(o_ref.dtype)


def _swa_attention(proj, positions, q_gain, k_gain, sinks):
    b, s, _ = proj.shape
    w = WINDOW
    assert s % w == 0 and TOK_W % (2 * SWA_KV_W) == 0
    kv_blk = TOK_W // (2 * SWA_KV_W)
    half = SWA_HD // 2
    inv_freq = ROPE_THETA ** (-jnp.arange(half, dtype=F32) / half)
    freq = jnp.tile(inv_freq, LANES // half).reshape(1, LANES)
    pos = positions.reshape(b, s, 1)
    grid_spec = pltpu.PrefetchScalarGridSpec(
        num_scalar_prefetch=1,
        grid=(b, s // w),
        in_specs=[
            pl.BlockSpec((None, w, TOK_W), lambda bi, i, sk: (bi, i, 0)),
            pl.BlockSpec((None, w, 2 * SWA_KV_W), lambda bi, i, sk: (bi, i, kv_blk)),
            pl.BlockSpec((None, w, 2 * SWA_KV_W), lambda bi, i, sk: (bi, jnp.maximum(i - 1, 0), kv_blk)),
            pl.BlockSpec((None, w, 1), lambda bi, i, sk: (bi, i, 0)),
            pl.BlockSpec((None, w, 1), lambda bi, i, sk: (bi, jnp.maximum(i - 1, 0), 0)),
            pl.BlockSpec((1, LANES), lambda bi, i, sk: (0, 0)),
            pl.BlockSpec((1, LANES), lambda bi, i, sk: (0, 0)),
            pl.BlockSpec((1, LANES), lambda bi, i, sk: (0, 0)),
        ],
        out_specs=pl.BlockSpec((None, w, TOK_W), lambda bi, i, sk: (bi, i, 0)),
        scratch_shapes=[pltpu.VMEM((w, LANES), F32), pltpu.VMEM((w, LANES), F32)],
    )
    return pl.pallas_call(
        _swa_kernel,
        out_shape=jax.ShapeDtypeStruct((b, s, TOK_W), BF16),
        grid_spec=grid_spec,
        compiler_params=_params(("parallel", "arbitrary")),
        name="swa_attention",
    )(sinks.astype(F32), proj, proj, proj, pos, pos, freq,
      jnp.tile(q_gain, 2).reshape(1, LANES), jnp.tile(k_gain, 2).reshape(1, LANES))


def _mix_out_kernel(x_ref, tok_ref, q0_ref, q1_ref, q2_ref, q3_ref, km_ref, vm_ref, qg_ref, wo_ref,
                    o_ref):
    acc = x_ref[...] + jnp.dot(tok_ref[...], wo_ref[:TOK_W, :], preferred_element_type=F32)
    memo = []
    for h, q_ref in enumerate((q0_ref, q1_ref, q2_ref, q3_ref)):
        cols = slice(h * MEM_HD, (h + 1) * MEM_HD)
        qn = _rms(q_ref[...].astype(F32), qg_ref[...]).astype(BF16)
        z = lax.dot_general(qn, km_ref[:, cols], NT_DIMS, preferred_element_type=F32) * (MEM_HD ** -0.5)
        e = jnp.exp(z - jnp.max(z, axis=-1, keepdims=True))
        p = e / jnp.sum(e, axis=-1, keepdims=True)
        memo.append(jnp.dot(p.astype(BF16), vm_ref[:, cols], preferred_element_type=F32).astype(BF16))
    memo = jnp.concatenate(memo, axis=1)
    o_ref[...] = acc + jnp.dot(memo, wo_ref[TOK_W:, :], preferred_element_type=F32)


def _mix_out(x, tok, proj, km, vm, q_gain, w_out, layer, *, memq_col, tm=512):
    b, s, d = x.shape
    assert s % tm == 0 and memq_col % MEM_HD == 0
    qblk = memq_col // MEM_HD
    q_specs = [pl.BlockSpec((None, tm, MEM_HD), functools.partial(lambda bi, i, h: (bi, i, qblk + h), h=h))
               for h in range(MEM_HEADS)]
    return pl.pallas_call(
        _mix_out_kernel,
        out_shape=jax.ShapeDtypeStruct((b, s, d), F32),
        grid=(b, s // tm),
        in_specs=[
            pl.BlockSpec((None, tm, d), lambda bi, i: (bi, i, 0)),
            pl.BlockSpec((None, tm, TOK_W), lambda bi, i: (bi, i, 0)),
            *q_specs,
            pl.BlockSpec((None, MEM_LEN, MEM_W), lambda bi, i: (bi, 0, 0)),
            pl.BlockSpec((None, MEM_LEN, MEM_W), lambda bi, i: (bi, 0, 0)),
            pl.BlockSpec((1, MEM_HD), lambda bi, i: (0, 0)),
            pl.BlockSpec((None, d, d), lambda bi, i: (layer, 0, 0)),
        ],
        out_specs=pl.BlockSpec((None, tm, d), lambda bi, i: (bi, i, 0)),
        compiler_params=_params(("parallel", "arbitrary")),
        name="mix_out",
    )(x, tok, proj, proj, proj, proj, km, vm, q_gain.reshape(1, MEM_HD), w_out)


def kernel(x, mem, positions, ffn_norm_g, ffn_w_gate, ffn_w_up, ffn_w_down, mix_norm_g, mem_norm_g,
           w_mem_kv, memq_norm_g, memk_norm_g, w_out, sb_w_in, swa_w_in, swa_q_norm_g, swa_k_norm_g,
           swa_sinks):
    b, s, d = x.shape
    depth = ffn_norm_g.shape[0]
    w_f32 = (ffn_w_gate, ffn_w_up, ffn_w_down)
    w_ffn = tuple(w[0, 0].astype(BF16) for w in w_f32)
    w_kv, w_o = w_mem_kv.astype(BF16), w_out.astype(BF16)
    for i in range(depth):
        x2, w_ffn = _ffn(x.reshape(b * s, d), ffn_norm_g[i, 0], w_ffn, w_f32, (i, 1))
        km, vm = _mem_kv(mem, mem_norm_g[i], w_kv, memk_norm_g[i], i)
        j = i // 2
        if i % 2 == 0:
            col_scale = jnp.where(jnp.arange(sb_w_in.shape[2]) < TOK_W, SB_HD ** -0.5, 1.0)
            w_in = (sb_w_in[j] * col_scale).astype(BF16)
            proj = _norm_proj(x2, mix_norm_g[i], w_in, tn=2560).reshape(b, s, -1)
            tok = _sb_attention(proj)
            memq_col = 3 * TOK_W
        else:
            w_in = swa_w_in[j].astype(BF16)
            proj = _norm_proj(x2, mix_norm_g[i], w_in, tn=w_in.shape[1]).reshape(b, s, -1)
            tok = _swa_attention(proj, positions, swa_q_norm_g[j], swa_k_norm_g[j], swa_sinks[j])
            memq_col = TOK_W + 2 * SWA_KV_W
        x3 = _mix_out(x2.reshape(b, s, d), tok, proj, km, vm, memq_norm_g[i], w_o, i,
                      memq_col=memq_col)
        x, w_ffn = _ffn(x3.reshape(b * s, d), ffn_norm_g[i, 1], w_ffn, w_f32,
                        (i + 1, 0) if i + 1 < depth else None)
        x = x.reshape(b, s, d)
    return x
```

```python
import functools

import jax
import jax.numpy as jnp
from jax import lax
from jax.experimental import pallas as pl
from jax.experimental.pallas import tpu as pltpu

D_MODEL = 2048
D_FF = 5632
MEM_LEN = 256
MEM_HEADS = 4
MEM_HD = 128
MEM_W = MEM_HEADS * MEM_HD
TOK_W = D_MODEL - MEM_W
SB_HD = 128
SB_HEADS = TOK_W // SB_HD
SWA_HD = 64
SWA_HEADS = TOK_W // SWA_HD
SWA_GROUP = 8
SWA_KV_HEADS = SWA_HEADS // SWA_GROUP
SWA_KV_W = SWA_KV_HEADS * SWA_HD
WINDOW = 128
ROPE_THETA = 10000.0
NORM_EPS = 1e-6
LOG2E = 1.4426950408889634

LANES = 128
VMEM_LIMIT = 58 * 1024 * 1024

F32 = jnp.float32
BF16 = jnp.bfloat16
NT_DIMS = (((1,), (1,)), ((), ()))
TN_DIMS = (((0,), (0,)), ((), ()))


def _params(semantics, flags=None):
    return pltpu.CompilerParams(dimension_semantics=semantics, vmem_limit_bytes=VMEM_LIMIT,
                                flags=flags)


def _rms(x, g):
    ms = jnp.mean(x * x, axis=-1, keepdims=True)
    return x * lax.rsqrt(ms + NORM_EPS) * g


def _ffn_kernel(x_ref, g_ref, wg_ref, wu_ref, wd_ref, *refs):
    n_cast = (len(refs) - 2) // 2
    cast_in, o_ref, cast_out, hn_ref = refs[:n_cast], refs[n_cast], refs[n_cast + 1:-1], refs[-1]

    def accumulate(base_ref):
        h = hn_ref[...]
        gate = jnp.dot(h, wg_ref[...], preferred_element_type=F32)
        up = jnp.dot(h, wu_ref[...], preferred_element_type=F32)
        act = (gate * jax.nn.sigmoid(gate)) * up * 0.5
        o_ref[...] = base_ref[...] + jnp.dot(act.astype(BF16), wd_ref[...],
                                             preferred_element_type=F32)

    @pl.when(pl.program_id(1) == 0)
    def _():
        hn_ref[...] = _rms(x_ref[...], g_ref[...]).astype(BF16)
        accumulate(x_ref)

    @pl.when(pl.program_id(1) > 0)
    def _():
        accumulate(o_ref)

    for src, dst in zip(cast_in, cast_out):
        dst[...] = src[...].astype(BF16)


def _ffn(x, g, w, w_f32=None, nxt=None, *, tm=1024, tf=512):
    n, d = x.shape
    f = w[0].shape[1]
    assert n % tm == 0 and f % tf == 0
    ni = n // tm
    in_specs = [
        pl.BlockSpec((tm, d), lambda i, j: (i, 0)),
        pl.BlockSpec((1, d), lambda i, j: (0, 0)),
        pl.BlockSpec((d, tf), lambda i, j: (0, j)),
        pl.BlockSpec((d, tf), lambda i, j: (0, j)),
        pl.BlockSpec((tf, d), lambda i, j: (j, 0)),
    ]
    out_shape = [jax.ShapeDtypeStruct((n, d), F32)]
    out_specs = [pl.BlockSpec((tm, d), lambda i, j: (i, 0))]
    cast = ()
    if nxt is not None:
        assert d % (16 * ni) == 0
        layer, half = nxt
        rows = d // ni
        cast = w_f32
        in_specs += [pl.BlockSpec((None, None, rows, tf), lambda i, j: (layer, half, i, j)),
                     pl.BlockSpec((None, None, rows, tf), lambda i, j: (layer, half, i, j)),
                     pl.BlockSpec((None, None, tf, rows), lambda i, j: (layer, half, j, i))]
        out_shape += [jax.ShapeDtypeStruct(c.shape[2:], BF16) for c in cast]
        out_specs += [pl.BlockSpec((rows, tf), lambda i, j: (i, j)),
                      pl.BlockSpec((rows, tf), lambda i, j: (i, j)),
                      pl.BlockSpec((tf, rows), lambda i, j: (j, i))]
    out = pl.pallas_call(
        _ffn_kernel,
        out_shape=out_shape,
        grid=(ni, f // tf),
        in_specs=in_specs,
        out_specs=out_specs,
        scratch_shapes=[pltpu.VMEM((tm, d), BF16)],
        compiler_params=_params(("parallel", "arbitrary")),
        name="ffn",
    )(x, g.reshape(1, d), *w, *cast)
    return out[0], tuple(out[1:])


def _norm_proj_kernel(x_ref, g_ref, w_ref, o_ref, hn_ref):
    def project():
        o_ref[...] = jnp.dot(hn_ref[...], w_ref[...],
                             preferred_element_type=F32).astype(o_ref.dtype)

    @pl.when(pl.program_id(1) == 0)
    def _():
        hn_ref[...] = _rms(x_ref[...], g_ref[...]).astype(BF16)
        project()

    @pl.when(pl.program_id(1) > 0)
    def _():
        project()


def _norm_proj(x, g, w, *, tm=1024, tn):
    n, d = x.shape
    wout = w.shape[1]
    assert n % tm == 0 and wout % tn == 0
    return pl.pallas_call(
        _norm_proj_kernel,
        out_shape=jax.ShapeDtypeStruct((n, wout), BF16),
        grid=(n // tm, wout // tn),
        in_specs=[
            pl.BlockSpec((tm, d), lambda i, j: (i, 0)),
            pl.BlockSpec((1, d), lambda i, j: (0, 0)),
            pl.BlockSpec((d, tn), lambda i, j: (0, j)),
        ],
        out_specs=pl.BlockSpec((tm, tn), lambda i, j: (i, j)),
        scratch_shapes=[pltpu.VMEM((tm, d), BF16)],
        compiler_params=_params(("parallel", "arbitrary")),
        name="norm_proj",
    )(x, g.reshape(1, d), w)


def _mem_kv_kernel(mem_ref, g_ref, w_ref, kg_ref, km_ref, vm_ref):
    mem_n = _rms(mem_ref[...], g_ref[...]).astype(BF16)
    kv = jnp.dot(mem_n, w_ref[...], preferred_element_type=F32)
    for h in range(MEM_HEADS):
        cols = slice(h * MEM_HD, (h + 1) * MEM_HD)
        km_ref[:, cols] = _rms(kv[:, cols], kg_ref[...]).astype(BF16)
    vm_ref[...] = kv[:, MEM_W:].astype(BF16)


def _mem_kv(mem, g, w, kg, layer):
    b, m, d = mem.shape
    out = jax.ShapeDtypeStruct((b, m, MEM_W), BF16)
    return pl.pallas_call(
        _mem_kv_kernel,
        out_shape=(out, out),
        grid=(b,),
        in_specs=[
            pl.BlockSpec((None, m, d), lambda i: (i, 0, 0)),
            pl.BlockSpec((1, d), lambda i: (0, 0)),
            pl.BlockSpec((None, d, 2 * MEM_W), lambda i: (layer, 0, 0)),
            pl.BlockSpec((1, MEM_HD), lambda i: (0, 0)),
        ],
        out_specs=(pl.BlockSpec((None, m, MEM_W), lambda i: (i, 0, 0)),
                   pl.BlockSpec((None, m, MEM_W), lambda i: (i, 0, 0))),
        compiler_params=_params(("parallel",)),
        name="mem_kv",
    )(mem, g.reshape(1, d), w, kg.reshape(1, MEM_HD))


SB_CARRY_DONE = 110.0


def _split_bf16(x):
    hi = x.astype(BF16)
    lo = (x - hi.astype(F32)).astype(BF16)
    return hi, lo


def _sb_kernel(q_ref, k_ref, v_ref, o_ref, *, blk, qw, heads):
    qi = pl.program_id(2)
    diag_blocks = qw // blk
    half = blk // 2
    row = lax.broadcasted_iota(jnp.int32, (half, half), 0)
    col = lax.broadcasted_iota(jnp.int32, (half, half), 1)
    suffix_ones = (col >= row).astype(BF16)
    suffix_ones2 = jnp.concatenate([suffix_ones, suffix_ones], axis=1)
    key_idx = lax.broadcasted_iota(jnp.int32, (half, qw), 0)
    query_idx = lax.broadcasted_iota(jnp.int32, (half, qw), 1)
    hs = range(heads)
    cols = [slice(h * SB_HD, (h + 1) * SB_HD) for h in hs]
    rows = [slice(0, half), slice(half, blk)]
    qs = [q_ref[:, cols[h]] for h in hs]

    def block(j, state, key_offset):
        start = pl.multiple_of(j * blk, blk)
        masked = key_offset is not None
        if masked:
            causal = [(key_idx + (key_offset + p * half)) < query_idx for p in range(2)]
        zs = [lax.dot_general(k_ref[pl.ds(start, blk), cols[h]], qs[h], NT_DIMS,
                              preferred_element_type=F32) for h in hs]
        zs = [[z[rows[p]] for p in range(2)] for z in zs]

        def suffix_sum(z, p):
            sp = jnp.maximum(z, 0.0) + jnp.log(1.0 + jnp.exp(-jnp.abs(z)))
            if masked:
                sp = jnp.where(causal[p], sp, 0.0)
            return jnp.dot(suffix_ones2, jnp.concatenate(_split_bf16(sp), axis=0),
                           preferred_element_type=F32)

        incls = [[suffix_sum(zh[p], p) for p in range(2)] for zh in zs]
        new_state = []
        for h in hs:
            carry, acc = state[h]
            late = jnp.exp(zs[h][1] - incls[h][1] - carry)
            carry = carry + incls[h][1][0:1, :]
            early = jnp.exp(zs[h][0] - incls[h][0] - carry)
            carry = carry + incls[h][0][0:1, :]
            if masked:
                early = jnp.where(causal[0], early, 0.0)
                late = jnp.where(causal[1], late, 0.0)
            probs = jnp.concatenate([early.astype(BF16), late.astype(BF16)], axis=0)
            acc = acc + lax.dot_general(v_ref[pl.ds(start, blk), cols[h]], probs, TN_DIMS,
                                        preferred_element_type=F32)
            new_state.append((carry, acc))
        return tuple(new_state)

    state = tuple((jnp.zeros((1, qw), F32), jnp.zeros((SB_HD, qw), F32)) for _ in hs)
    first = qi * diag_blocks
    for r in reversed(range(diag_blocks)):
        state = block(first + r, state, r * blk)

    def unfinished(st):
        least = functools.reduce(jnp.minimum, [st[h][0] for h in hs])
        return (jnp.min(least) < SB_CARRY_DONE).astype(jnp.int32)

    def sweep(c):
        i, _, st = c
        st = block(first - 1 - i, st, None)
        return i + 1, unfinished(st), st

    _, _, state = lax.while_loop(lambda c: jnp.logical_and(c[0] < first, c[1] > 0), sweep,
                                 (jnp.int32(0), unfinished(state), state))
    for h in hs:
        o_ref[:, cols[h]] = state[h][1].T.astype(o_ref.dtype)


def _sb_attention(proj, *, blk=256, qw=256, heads=6):
    b, s, _ = proj.shape
    gw = heads * SB_HD
    groups = TOK_W // gw
    assert TOK_W % gw == 0 and s % qw == 0 and qw % blk == 0
    return pl.pallas_call(
        functools.partial(_sb_kernel, blk=blk, qw=qw, heads=heads),
        out_shape=jax.ShapeDtypeStruct((b, s, TOK_W), BF16),
        grid=(b, groups, s // qw),
        in_specs=[
            pl.BlockSpec((None, qw, gw), lambda bi, g, i: (bi, i, g)),
            pl.BlockSpec((None, s, gw), lambda bi, g, i: (bi, 0, groups + g)),
            pl.BlockSpec((None, s, gw), lambda bi, g, i: (bi, 0, 2 * groups + g)),
        ],
        out_specs=pl.BlockSpec((None, qw, gw), lambda bi, g, i: (bi, i, g)),
        compiler_params=_params(("parallel", "parallel", "arbitrary")),
        name="sb_attention",
    )(proj, proj, proj)


def _rope(x, cos, sin_signed, first_half):
    partner = jnp.where(first_half, pltpu.roll(x, 96, axis=1), pltpu.roll(x, 32, axis=1))
    return x * cos + partner * sin_signed


def _head_rms(x, pair_ones, g):
    hi, lo = _split_bf16(x * x)
    ss = (jnp.dot(hi, pair_ones, preferred_element_type=F32)
          + jnp.dot(lo, pair_ones, preferred_element_type=F32))
    return x * lax.rsqrt(ss * (1.0 / SWA_HD) + NORM_EPS) * g


def _swa_kernel(sinks_ref, q_ref, kvc_ref, kvp_ref, posc_ref, posp_ref, freq_ref, qg_ref, kg_ref,
                o_ref, cosp_ref, sinp_ref):
    n = pl.program_id(1)
    w = WINDOW
    lane = lax.broadcasted_iota(jnp.int32, (1, LANES), 1)
    first_half = (lane & (SWA_HD - 1)) < (SWA_HD // 2)
    low = lane < SWA_HD
    r = lax.broadcasted_iota(jnp.int32, (LANES, LANES), 0)
    c = lax.broadcasted_iota(jnp.int32, (LANES, LANES), 1)
    pair_ones = ((r < SWA_HD) == (c < SWA_HD)).astype(BF16)

    def tables(pos_ref):
        ang = pos_ref[...].astype(F32) * freq_ref[...]
        sin = jnp.sin(ang)
        return jnp.cos(ang), jnp.where(first_half, -sin, sin)

    cos_c, sin_c = tables(posc_ref)

    @pl.when(n == 0)
    def _():
        cos_p0, sin_p0 = tables(posp_ref)
        cosp_ref[...] = cos_p0
        sinp_ref[...] = sin_p0

    cos_p, sin_p = cosp_ref[...], sinp_ref[...]
    cosp_ref[...] = cos_c
    sinp_ref[...] = sin_c
    cos_k = jnp.concatenate([cos_p, cos_c], axis=0)
    sin_k = jnp.concatenate([sin_p, sin_c], axis=0)

    n_chunks = TOK_W // LANES
    q = q_ref[...].astype(F32)
    qs = jnp.concatenate([q[:, i * LANES:(i + 1) * LANES] for i in range(n_chunks)], axis=0)
    qs = _head_rms(qs, pair_ones, qg_ref[...] * (LOG2E * SWA_HD ** -0.5))
    qs = _rope(qs, jnp.concatenate([cos_c] * n_chunks, axis=0),
               jnp.concatenate([sin_c] * n_chunks, axis=0), first_half).astype(BF16)

    kv = jnp.concatenate([kvp_ref[...], kvc_ref[...]], axis=0).astype(F32)
    kn = [_rope(_head_rms(kv[:, i * LANES:(i + 1) * LANES], pair_ones, kg_ref[...]),
                cos_k, sin_k, first_half) for i in range(2)]

    def block_diag(x, in_low):
        sw = pltpu.roll(x, SWA_HD, axis=1)
        top = jnp.where(low, x if in_low else sw, 0.0)
        bot = jnp.where(low, 0.0, sw if in_low else x)
        return jnp.concatenate([top, bot], axis=0).astype(BF16)

    k_src = [(kn[0], True), (kn[0], False), (kn[1], True)]
    v_src = [(kv[:, LANES:2 * LANES], False), (kv[:, 2 * LANES:], True), (kv[:, 2 * LANES:], False)]

    qpos = lax.broadcasted_iota(jnp.int32, (w, 2 * w), 0)
    kpos = lax.broadcasted_iota(jnp.int32, (w, 2 * w), 1)
    dist = qpos + w - kpos
    valid = (dist >= 0) & (dist < w) & ((kpos >= w) | (n > 0))

    kv_heads = range(SWA_KV_HEADS)
    zs = [lax.dot_general(qs[h * 4 * w:(h + 1) * 4 * w], block_diag(*k_src[h]), NT_DIMS,
                          preferred_element_type=F32) for h in kv_heads]

    def softmax(z, h):
        rows, scales = [], []
        for ci in range(4):
            halves, recips = [], []
            for p in range(2):
                sink = sinks_ref[h * SWA_GROUP + 2 * ci + p] * LOG2E
                zb = jnp.where(valid, z[ci * w:(ci + 1) * w, p * 2 * w:(p + 1) * 2 * w], -jnp.inf)
                m = jnp.maximum(jnp.max(zb, axis=-1, keepdims=True), sink)
                e = jnp.exp2(zb - m)
                recips.append(1.0 / (jnp.sum(e, axis=-1, keepdims=True) + jnp.exp2(sink - m)))
                halves.append(e.astype(BF16))
            rows.append(jnp.concatenate(halves, axis=1))
            scales.append(jnp.where(low, recips[0], recips[1]))
        return jnp.concatenate(rows, axis=0), scales

    ps = [softmax(zs[h], h) for h in kv_heads]
    outs = [jnp.dot(ps[h][0], block_diag(*v_src[h]), preferred_element_type=F32) for h in kv_heads]
    for h in kv_heads:
        for ci in range(4):
            cols = slice((h * 4 + ci) * LANES, (h * 4 + ci + 1) * LANES)
            o_ref[:, cols] = (outs[h][ci * w:(ci + 1) * w] * ps[h][1][ci]).astype(o_ref.dtype)


def _swa_attention(proj, positions, q_gain, k_gain, sinks):
    b, s, _ = proj.shape
    w = WINDOW
    assert s % w == 0 and TOK_W % (2 * SWA_KV_W) == 0
    kv_blk = TOK_W // (2 * SWA_KV_W)
    half = SWA_HD // 2
    inv_freq = ROPE_THETA ** (-jnp.arange(half, dtype=F32) / half)
    freq = jnp.tile(inv_freq, LANES // half).reshape(1, LANES)
    pos = positions.reshape(b, s, 1)
    grid_spec = pltpu.PrefetchScalarGridSpec(
        num_scalar_prefetch=1,
        grid=(b, s // w),
        in_specs=[
            pl.BlockSpec((None, w, TOK_W), lambda bi, i, sk: (bi, i, 0)),
            pl.BlockSpec((None, w, 2 * SWA_KV_W), lambda bi, i, sk: (bi, i, kv_blk)),
            pl.BlockSpec((None, w, 2 * SWA_KV_W), lambda bi, i, sk: (bi, jnp.maximum(i - 1, 0), kv_blk)),
            pl.BlockSpec((None, w, 1), lambda bi, i, sk: (bi, i, 0)),
            pl.BlockSpec((None, w, 1), lambda bi, i, sk: (bi, jnp.maximum(i - 1, 0), 0)),
            pl.BlockSpec((1, LANES), lambda bi, i, sk: (0, 0)),
            pl.BlockSpec((1, LANES), lambda bi, i, sk: (0, 0)),
            pl.BlockSpec((1, LANES), lambda bi, i, sk: (0, 0)),
        ],
        out_specs=pl.BlockSpec((None, w, TOK_W), lambda bi, i, sk: (bi, i, 0)),
        scratch_shapes=[pltpu.VMEM((w, LANES), F32), pltpu.VMEM((w, LANES), F32)],
    )
    return pl.pallas_call(
        _swa_kernel,
        out_shape=jax.ShapeDtypeStruct((b, s, TOK_W), BF16),
        grid_spec=grid_spec,
        compiler_params=_params(("parallel", "arbitrary")),
        name="swa_attention",
    )(sinks.astype(F32), proj, proj, proj, pos, pos, freq,
      jnp.tile(q_gain, 2).reshape(1, LANES), jnp.tile(k_gain, 2).reshape(1, LANES))


def _mix_out_kernel(x_ref, tok_ref, q0_ref, q1_ref, q2_ref, q3_ref, km_ref, vm_ref, qg_ref, wo_ref,
                    o_ref):
    acc = x_ref[...] + jnp.dot(tok_ref[...], wo_ref[:TOK_W, :], preferred_element_type=F32)
    memo = []
    for h, q_ref in enumerate((q0_ref, q1_ref, q2_ref, q3_ref)):
        cols = slice(h * MEM_HD, (h + 1) * MEM_HD)
        qn = _rms(q_ref[...].astype(F32), qg_ref[...]).astype(BF16)
        z = lax.dot_general(qn, km_ref[:, cols], NT_DIMS, preferred_element_type=F32) * (MEM_HD ** -0.5)
        e = jnp.exp(z - jnp.max(z, axis=-1, keepdims=True))
        p = e / jnp.sum(e, axis=-1, keepdims=True)
        memo.append(jnp.dot(p.astype(BF16), vm_ref[:, cols], preferred_element_type=F32).astype(BF16))
    memo = jnp.concatenate(memo, axis=1)
    o_ref[...] = acc + jnp.dot(memo, wo_ref[TOK_W:, :], preferred_element_type=F32)


def _mix_out(x, tok, proj, km, vm, q_gain, w_out, layer, *, memq_col, tm=512):
    b, s, d = x.shape
    assert s % tm == 0 and memq_col % MEM_HD == 0
    qblk = memq_col // MEM_HD
    q_specs = [pl.BlockSpec((None, tm, MEM_HD), functools.partial(lambda bi, i, h: (bi, i, qblk + h), h=h))
               for h in range(MEM_HEADS)]
    return pl.pallas_call(
        _mix_out_kernel,
        out_shape=jax.ShapeDtypeStruct((b, s, d), F32),
        grid=(b, s // tm),
        in_specs=[
            pl.BlockSpec((None, tm, d), lambda bi, i: (bi, i, 0)),
            pl.BlockSpec((None, tm, TOK_W), lambda bi, i: (bi, i, 0)),
            *q_specs,
            pl.BlockSpec((None, MEM_LEN, MEM_W), lambda bi, i: (bi, 0, 0)),
            pl.BlockSpec((None, MEM_LEN, MEM_W), lambda bi, i: (bi, 0, 0)),
            pl.BlockSpec((1, MEM_HD), lambda bi, i: (0, 0)),
            pl.BlockSpec((None, d, d), lambda bi, i: (layer, 0, 0)),
        ],
        out_specs=pl.BlockSpec((None, tm, d), lambda bi, i: (bi, i, 0)),
        compiler_params=_params(("parallel", "arbitrary")),
        name="mix_out",
    )(x, tok, proj, proj, proj, proj, km, vm, q_gain.reshape(1, MEM_HD), w_out)


def kernel(x, mem, positions, ffn_norm_g, ffn_w_gate, ffn_w_up, ffn_w_down, mix_norm_g, mem_norm_g,
           w_mem_kv, memq_norm_g, memk_norm_g, w_out, sb_w_in, swa_w_in, swa_q_norm_g, swa_k_norm_g,
           swa_sinks):
    b, s, d = x.shape
    depth = ffn_norm_g.shape[0]
    w_f32 = (ffn_w_gate, ffn_w_up, ffn_w_down)
    w_ffn = tuple(w[0, 0].astype(BF16) for w in w_f32)
    w_kv, w_o = w_mem_kv.astype(BF16), w_out.astype(BF16)
    for i in range(depth):
        x2, w_ffn = _ffn(x.reshape(b * s, d), ffn_norm_g[i, 0], w_ffn, w_f32, (i, 1))
        km, vm = _mem_kv(mem, mem_norm_g[i], w_kv, memk_norm_g[i], i)
        j = i // 2
        if i % 2 == 0:
            col_scale = jnp.where(jnp.arange(sb_w_in.shape[2]) < TOK_W, SB_HD ** -0.5, 1.0)
            w_in = (sb_w_in[j] * col_scale).astype(BF16)
            proj = _norm_proj(x2, mix_norm_g[i], w_in, tn=2560).reshape(b, s, -1)
            tok = _sb_attention(proj)
            memq_col = 3 * TOK_W
        else:
            w_in = swa_w_in[j].astype(BF16)
            proj = _norm_proj(x2, mix_norm_g[i], w_in, tn=w_in.shape[1]).reshape(b, s, -1)
            tok = _swa_attention(proj, positions, swa_q_norm_g[j], swa_k_norm_g[j], swa_sinks[j])
            memq_col = TOK_W + 2 * SWA_KV_W
        x3 = _mix_out(x2.reshape(b, s, d), tok, proj, km, vm, memq_norm_g[i], w_o, i,
                      memq_col=memq_col)
        x, w_ffn = _ffn(x3.reshape(b * s, d), ffn_norm_g[i, 1], w_ffn, w_f32,
                        (i + 1, 0) if i + 1 < depth else None)
        x = x.reshape(b, s, d)
    return x
```

```python
import functools

import jax
import jax.numpy as jnp
from jax import lax
from jax.experimental import pallas as pl
from jax.experimental.pallas import tpu as pltpu

D_MODEL = 2048
D_FF = 5632
MEM_LEN = 256
MEM_HEADS = 4
MEM_HD = 128
MEM_W = MEM_HEADS * MEM_HD
TOK_W = D_MODEL - MEM_W
SB_HD = 128
SB_HEADS = TOK_W // SB_HD
SWA_HD = 64
SWA_HEADS = TOK_W // SWA_HD
SWA_GROUP = 8
SWA_KV_HEADS = SWA_HEADS // SWA_GROUP
SWA_KV_W = SWA_KV_HEADS * SWA_HD
WINDOW = 128
ROPE_THETA = 10000.0
NORM_EPS = 1e-6
LOG2E = 1.4426950408889634

LANES = 128
VMEM_LIMIT = 58 * 1024 * 1024

F32 = jnp.float32
BF16 = jnp.bfloat16
NT_DIMS = (((1,), (1,)), ((), ()))
TN_DIMS = (((0,), (0,)), ((), ()))


def _params(semantics, flags=None):
    return pltpu.CompilerParams(dimension_semantics=semantics, vmem_limit_bytes=VMEM_LIMIT,
                                flags=flags)


def _rms(x, g):
    ms = jnp.mean(x * x, axis=-1, keepdims=True)
    return x * lax.rsqrt(ms + NORM_EPS) * g


def _ffn_kernel(x_ref, g_ref, wgu_ref, wd_ref, *refs):
    if len(refs) == 2:
        (o_ref, hn_ref), cast = refs, None
    else:
        g32_ref, u32_ref, d32_ref, o_ref, gu_out_ref, d_out_ref, hn_ref = refs
        cast = True
    tf = wd_ref.shape[0]

    def accumulate(base_ref):
        gu = jnp.dot(hn_ref[...], wgu_ref[...], preferred_element_type=F32)
        gate, up = gu[:, :tf], gu[:, tf:]
        act = (gate * jax.nn.sigmoid(gate)) * up * 0.5
        o_ref[...] = base_ref[...] + jnp.dot(act.astype(BF16), wd_ref[...],
                                             preferred_element_type=F32)

    @pl.when(pl.program_id(1) == 0)
    def _():
        hn_ref[...] = _rms(x_ref[...], g_ref[...]).astype(BF16)
        accumulate(x_ref)

    @pl.when(pl.program_id(1) > 0)
    def _():
        accumulate(o_ref)

    if cast:
        gu_out_ref[:, :tf] = g32_ref[...].astype(BF16)
        gu_out_ref[:, tf:] = u32_ref[...].astype(BF16)
        d_out_ref[...] = d32_ref[...].astype(BF16)


def _interleave_gate_up(wg, wu, tf):
    d, f = wg.shape
    return jnp.stack([wg.reshape(d, f // tf, tf), wu.reshape(d, f // tf, tf)], axis=2).reshape(d, 2 * f)


def _ffn(x, g, w, w_f32=None, nxt=None, *, tm=1024, tf=512):
    n, d = x.shape
    f = w[1].shape[0]
    assert n % tm == 0 and f % tf == 0
    ni = n // tm
    in_specs = [
        pl.BlockSpec((tm, d), lambda i, j: (i, 0)),
        pl.BlockSpec((1, d), lambda i, j: (0, 0)),
        pl.BlockSpec((d, 2 * tf), lambda i, j: (0, j)),
        pl.BlockSpec((tf, d), lambda i, j: (j, 0)),
    ]
    out_shape = [jax.ShapeDtypeStruct((n, d), F32)]
    out_specs = [pl.BlockSpec((tm, d), lambda i, j: (i, 0))]
    cast = ()
    if nxt is not None:
        assert d % (16 * ni) == 0
        layer, half = nxt
        rows = d // ni
        cast = w_f32
        in_specs += [pl.BlockSpec((None, None, rows, tf), lambda i, j: (layer, half, i, j)),
                     pl.BlockSpec((None, None, rows, tf), lambda i, j: (layer, half, i, j)),
                     pl.BlockSpec((None, None, tf, rows), lambda i, j: (layer, half, j, i))]
        out_shape += [jax.ShapeDtypeStruct((d, 2 * f), BF16), jax.ShapeDtypeStruct((f, d), BF16)]
        out_specs += [pl.BlockSpec((rows, 2 * tf), lambda i, j: (i, j)),
                      pl.BlockSpec((tf, rows), lambda i, j: (j, i))]
    out = pl.pallas_call(
        _ffn_kernel,
        out_shape=out_shape,
        grid=(ni, f // tf),
        in_specs=in_specs,
        out_specs=out_specs,
        scratch_shapes=[pltpu.VMEM((tm, d), BF16)],
        compiler_params=_params(("parallel", "arbitrary")),
        name="ffn",
    )(x, g.reshape(1, d), *w, *cast)
    return out[0], tuple(out[1:])


def _norm_proj_kernel(x_ref, g_ref, w_ref, o_ref, hn_ref):
    def project():
        o_ref[...] = jnp.dot(hn_ref[...], w_ref[...],
                             preferred_element_type=F32).astype(o_ref.dtype)

    @pl.when(pl.program_id(1) == 0)
    def _():
        hn_ref[...] = _rms(x_ref[...], g_ref[...]).astype(BF16)
        project()

    @pl.when(pl.program_id(1) > 0)
    def _():
        project()


def _norm_proj(x, g, w, *, tm=1024, tn):
    n, d = x.shape
    wout = w.shape[1]
    assert n % tm == 0 and wout % tn == 0
    return pl.pallas_call(
        _norm_proj_kernel,
        out_shape=jax.ShapeDtypeStruct((n, wout), BF16),
        grid=(n // tm, wout // tn),
        in_specs=[
            pl.BlockSpec((tm, d), lambda i, j: (i, 0)),
            pl.BlockSpec((1, d), lambda i, j: (0, 0)),
            pl.BlockSpec((d, tn), lambda i, j: (0, j)),
        ],
        out_specs=pl.BlockSpec((tm, tn), lambda i, j: (i, j)),
        scratch_shapes=[pltpu.VMEM((tm, d), BF16)],
        compiler_params=_params(("parallel", "arbitrary")),
        name="norm_proj",
    )(x, g.reshape(1, d), w)


def _mem_kv_kernel(mem_ref, g_ref, w_ref, kg_ref, km_ref, vm_ref):
    mem_n = _rms(mem_ref[...], g_ref[...]).astype(BF16)
    kv = jnp.dot(mem_n, w_ref[...], preferred_element_type=F32)
    for h in range(MEM_HEADS):
        cols = slice(h * MEM_HD, (h + 1) * MEM_HD)
        km_ref[:, cols] = _rms(kv[:, cols], kg_ref[...]).astype(BF16)
    vm_ref[...] = kv[:, MEM_W:].astype(BF16)


def _mem_kv(mem, g, w, kg, layer):
    b, m, d = mem.shape
    out = jax.ShapeDtypeStruct((b, m, MEM_W), BF16)
    return pl.pallas_call(
        _mem_kv_kernel,
        out_shape=(out, out),
        grid=(b,),
        in_specs=[
            pl.BlockSpec((None, m, d), lambda i: (i, 0, 0)),
            pl.BlockSpec((1, d), lambda i: (0, 0)),
            pl.BlockSpec((None, d, 2 * MEM_W), lambda i: (layer, 0, 0)),
            pl.BlockSpec((1, MEM_HD), lambda i: (0, 0)),
        ],
        out_specs=(pl.BlockSpec((None, m, MEM_W), lambda i: (i, 0, 0)),
                   pl.BlockSpec((None, m, MEM_W), lambda i: (i, 0, 0))),
        compiler_params=_params(("parallel",)),
        name="mem_kv",
    )(mem, g.reshape(1, d), w, kg.reshape(1, MEM_HD))


SB_CARRY_DONE = 110.0


def _split_bf16(x):
    hi = x.astype(BF16)
    lo = (x - hi.astype(F32)).astype(BF16)
    return hi, lo


def _sb_kernel(q_ref, k_ref, v_ref, o_ref, *, blk, qw, heads):
    qi = pl.program_id(2)
    diag_blocks = qw // blk
    half = blk // 2
    row = lax.broadcasted_iota(jnp.int32, (half, half), 0)
    col = lax.broadcasted_iota(jnp.int32, (half, half), 1)
    suffix_ones = (col >= row).astype(BF16)
    suffix_ones2 = jnp.concatenate([suffix_ones, suffix_ones], axis=1)
    key_idx = lax.broadcasted_iota(jnp.int32, (half, qw), 0)
    query_idx = lax.broadcasted_iota(jnp.int32, (half, qw), 1)
    hs = range(heads)
    cols = [slice(h * SB_HD, (h + 1) * SB_HD) for h in hs]
    rows = [slice(0, half), slice(half, blk)]
    qs = [q_ref[:, cols[h]] for h in hs]

    def block(j, state, key_offset):
        start = pl.multiple_of(j * blk, blk)
        masked = key_offset is not None
        if masked:
            causal = [(key_idx + (key_offset + p * half)) < query_idx for p in range(2)]
        zs = [lax.dot_general(k_ref[pl.ds(start, blk), cols[h]], qs[h], NT_DIMS,
                              preferred_element_type=F32) for h in hs]
        zs = [[z[rows[p]] for p in range(2)] for z in zs]

        def suffix_sum(z, p):
            sp = jnp.maximum(z, 0.0) + jnp.log(1.0 + jnp.exp(-jnp.abs(z)))
            if masked:
                sp = jnp.where(causal[p], sp, 0.0)
            return jnp.dot(suffix_ones2, jnp.concatenate(_split_bf16(sp), axis=0),
                           preferred_element_type=F32)

        incls = [[suffix_sum(zh[p], p) for p in range(2)] for zh in zs]
        new_state = []
        for h in hs:
            carry, acc = state[h]
            late = jnp.exp(zs[h][1] - incls[h][1] - carry)
            carry = carry + incls[h][1][0:1, :]
            early = jnp.exp(zs[h][0] - incls[h][0] - carry)
            carry = carry + incls[h][0][0:1, :]
            if masked:
                early = jnp.where(causal[0], early, 0.0)
                late = jnp.where(causal[1], late, 0.0)
            probs = jnp.concatenate([early.astype(BF16), late.astype(BF16)], axis=0)
            acc = acc + lax.dot_general(v_ref[pl.ds(start, blk), cols[h]], probs, TN_DIMS,
                                        preferred_element_type=F32)
            new_state.append((carry, acc))
        return tuple(new_state)

    state = tuple((jnp.zeros((1, qw), F32), jnp.zeros((SB_HD, qw), F32)) for _ in hs)
    first = qi * diag_blocks
    for r in reversed(range(diag_blocks)):
        state = block(first + r, state, r * blk)

    def unfinished(st):
        least = functools.reduce(jnp.minimum, [st[h][0] for h in hs])
        return (jnp.min(least) < SB_CARRY_DONE).astype(jnp.int32)

    def sweep(c):
        i, _, st = c
        st = block(first - 1 - i, st, None)
        return i + 1, unfinished(st), st

    _, _, state = lax.while_loop(lambda c: jnp.logical_and(c[0] < first, c[1] > 0), sweep,
                                 (jnp.int32(0), unfinished(state), state))
    for h in hs:
        o_ref[:, cols[h]] = state[h][1].T.astype(o_ref.dtype)


def _sb_attention(proj, *, blk=256, qw=256, heads=6):
    b, s, _ = proj.shape
    gw = heads * SB_HD
    groups = TOK_W // gw
    assert TOK_W % gw == 0 and s % qw == 0 and qw % blk == 0
    return pl.pallas_call(
        functools.partial(_sb_kernel, blk=blk, qw=qw, heads=heads),
        out_shape=jax.ShapeDtypeStruct((b, s, TOK_W), BF16),
        grid=(b, groups, s // qw),
        in_specs=[
            pl.BlockSpec((None, qw, gw), lambda bi, g, i: (bi, i, g)),
            pl.BlockSpec((None, s, gw), lambda bi, g, i: (bi, 0, groups + g)),
            pl.BlockSpec((None, s, gw), lambda bi, g, i: (bi, 0, 2 * groups + g)),
        ],
        out_specs=pl.BlockSpec((None, qw, gw), lambda bi, g, i: (bi, i, g)),
        compiler_params=_params(("parallel", "parallel", "arbitrary")),
        name="sb_attention",
    )(proj, proj, proj)


def _rope(x, cos, sin_signed, first_half):
    partner = jnp.where(first_half, pltpu.roll(x, 96, axis=1), pltpu.roll(x, 32, axis=1))
    return x * cos + partner * sin_signed


def _head_rms(x, pair_ones, g):
    hi, lo = _split_bf16(x * x)
    ss = (jnp.dot(hi, pair_ones, preferred_element_type=F32)
          + jnp.dot(lo, pair_ones, preferred_element_type=F32))
    return x * lax.rsqrt(ss * (1.0 / SWA_HD) + NORM_EPS) * g


def _swa_kernel(sinks_ref, q_ref, kvc_ref, kvp_ref, posc_ref, posp_ref, freq_ref, qg_ref, kg_ref,
                o_ref, cosp_ref, sinp_ref):
    n = pl.program_id(1)
    w = WINDOW
    lane = lax.broadcasted_iota(jnp.int32, (1, LANES), 1)
    first_half = (lane & (SWA_HD - 1)) < (SWA_HD // 2)
    low = lane < SWA_HD
    r = lax.broadcasted_iota(jnp.int32, (LANES, LANES), 0)
    c = lax.broadcasted_iota(jnp.int32, (LANES, LANES), 1)
    pair_ones = ((r < SWA_HD) == (c < SWA_HD)).astype(BF16)

    def tables(pos_ref):
        ang = pos_ref[...].astype(F32) * freq_ref[...]
        sin = jnp.sin(ang)
        return jnp.cos(ang), jnp.where(first_half, -sin, sin)

    cos_c, sin_c = tables(posc_ref)

    @pl.when(n == 0)
    def _():
        cos_p0, sin_p0 = tables(posp_ref)
        cosp_ref[...] = cos_p0
        sinp_ref[...] = sin_p0

    cos_p, sin_p = cosp_ref[...], sinp_ref[...]
    cosp_ref[...] = cos_c
    sinp_ref[...] = sin_c
    cos_k = jnp.concatenate([cos_p, cos_c], axis=0)
    sin_k = jnp.concatenate([sin_p, sin_c], axis=0)

    n_chunks = TOK_W // LANES
    q = q_ref[...].astype(F32)
    qs = jnp.concatenate([q[:, i * LANES:(i + 1) * LANES] for i in range(n_chunks)], axis=0)
    qs = _head_rms(qs, pair_ones, qg_ref[...] * (LOG2E * SWA_HD ** -0.5))
    qs = _rope(qs, jnp.concatenate([cos_c] * n_chunks, axis=0),
               jnp.concatenate([sin_c] * n_chunks, axis=0), first_half).astype(BF16)

    kv = jnp.concatenate([kvp_ref[...], kvc_ref[...]], axis=0).astype(F32)
    kn = [_rope(_head_rms(kv[:, i * LANES:(i + 1) * LANES], pair_ones, kg_ref[...]),
                cos_k, sin_k, first_half) for i in range(2)]

    def block_diag(x, in_low):
        sw = pltpu.roll(x, SWA_HD, axis=1)
        top = jnp.where(low, x if in_low else sw, 0.0)
        bot = jnp.where(low, 0.0, sw if in_low else x)
        return jnp.concatenate([top, bot], axis=0).astype(BF16)

    k_src = [(kn[0], True), (kn[0], False), (kn[1], True)]
    v_src = [(kv[:, LANES:2 * LANES], False), (kv[:, 2 * LANES:], True), (kv[:, 2 * LANES:], False)]

    qpos = lax.broadcasted_iota(jnp.int32, (w, 2 * w), 0)
    kpos = lax.broadcasted_iota(jnp.int32, (w, 2 * w), 1)
    dist = qpos + w - kpos
    valid = (dist >= 0) & (dist < w) & ((kpos >= w) | (n > 0))

    kv_heads = range(SWA_KV_HEADS)
    zs = [lax.dot_general(qs[h * 4 * w:(h + 1) * 4 * w], block_diag(*k_src[h]), NT_DIMS,
                          preferred_element_type=F32) for h in kv_heads]

    def softmax(z, h):
        rows, scales = [], []
        for ci in range(4):
            halves, recips = [], []
            for p in range(2):
                sink = sinks_ref[h * SWA_GROUP + 2 * ci + p] * LOG2E
                zb = jnp.where(valid, z[ci * w:(ci + 1) * w, p * 2 * w:(p + 1) * 2 * w], -jnp.inf)
                m = jnp.maximum(jnp.max(zb, axis=-1, keepdims=True), sink)
                e = jnp.exp2(zb - m)
                recips.append(1.0 / (jnp.sum(e, axis=-1, keepdims=True) + jnp.exp2(sink - m)))
                halves.append(e.astype(BF16))
            rows.append(jnp.concatenate(halves, axis=1))
            scales.append(jnp.where(low, recips[0], recips[1]))
        return jnp.concatenate(rows, axis=0), scales

    ps = [softmax(zs[h], h) for h in kv_heads]
    outs = [jnp.dot(ps[h][0], block_diag(*v_src[h]), preferred_element_type=F32) for h in kv_heads]
    for h in kv_heads:
        for ci in range(4):
            cols = slice((h * 4 + ci) * LANES, (h * 4 + ci + 1) * LANES)
            o_ref[:, cols] = (outs[h][ci * w:(ci + 1) * w] * ps[h][1][ci]).astype(o_ref.dtype)


def _swa_attention(proj, positions, q_gain, k_gain, sinks):
    b, s, _ = proj.shape
    w = WINDOW
    assert s % w == 0 and TOK_W % (2 * SWA_KV_W) == 0
    kv_blk = TOK_W // (2 * SWA_KV_W)
    half = SWA_HD // 2
    inv_freq = ROPE_THETA ** (-jnp.arange(half, dtype=F32) / half)
    freq = jnp.tile(inv_freq, LANES // half).reshape(1, LANES)
    pos = positions.reshape(b, s, 1)
    grid_spec = pltpu.PrefetchScalarGridSpec(
        num_scalar_prefetch=1,
        grid=(b, s // w),
        in_specs=[
            pl.BlockSpec((None, w, TOK_W), lambda bi, i, sk: (bi, i, 0)),
            pl.BlockSpec((None, w, 2 * SWA_KV_W), lambda bi, i, sk: (bi, i, kv_blk)),
            pl.BlockSpec((None, w, 2 * SWA_KV_W), lambda bi, i, sk: (bi, jnp.maximum(i - 1, 0), kv_blk)),
            pl.BlockSpec((None, w, 1), lambda bi, i, sk: (bi, i, 0)),
            pl.BlockSpec((None, w, 1), lambda bi, i, sk: (bi, jnp.maximum(i - 1, 0), 0)),
            pl.BlockSpec((1, LANES), lambda bi, i, sk: (0, 0)),
            pl.BlockSpec((1, LANES), lambda bi, i, sk: (0, 0)),
            pl.BlockSpec((1, LANES), lambda bi, i, sk: (0, 0)),
        ],
        out_specs=pl.BlockSpec((None, w, TOK_W), lambda bi, i, sk: (bi, i, 0)),
        scratch_shapes=[pltpu.VMEM((w, LANES), F32), pltpu.VMEM((w, LANES), F32)],
    )
    return pl.pallas_call(
        _swa_kernel,
        out_shape=jax.ShapeDtypeStruct((b, s, TOK_W), BF16),
        grid_spec=grid_spec,
        compiler_params=_params(("parallel", "arbitrary")),
        name="swa_attention",
    )(sinks.astype(F32), proj, proj, proj, pos, pos, freq,
      jnp.tile(q_gain, 2).reshape(1, LANES), jnp.tile(k_gain, 2).reshape(1, LANES))


def _mix_out_kernel(x_ref, tok_ref, q0_ref, q1_ref, q2_ref, q3_ref, km_ref, vm_ref, qg_ref, wo_ref,
                    o_ref):
    acc = x_ref[...] + jnp.dot(tok_ref[...], wo_ref[:TOK_W, :], preferred_element_type=F32)
    memo = []
    for h, q_ref in enumerate((q0_ref, q1_ref, q2_ref, q3_ref)):
        cols = slice(h * MEM_HD, (h + 1) * MEM_HD)
        qn = _rms(q_ref[...].astype(F32), qg_ref[...]).astype(BF16)
        z = lax.dot_general(qn, km_ref[:, cols], NT_DIMS, preferred_element_type=F32) * (MEM_HD ** -0.5)
        e = jnp.exp(z - jnp.max(z, axis=-1, keepdims=True))
        p = e / jnp.sum(e, axis=-1, keepdims=True)
        memo.append(jnp.dot(p.astype(BF16), vm_ref[:, cols], preferred_element_type=F32).astype(BF16))
    memo = jnp.concatenate(memo, axis=1)
    o_ref[...] = acc + jnp.dot(memo, wo_ref[TOK_W:, :], preferred_element_type=F32)


def _mix_out(x, tok, proj, km, vm, q_gain, w_out, layer, *, memq_col, tm=512):
    b, s, d = x.shape
    assert s % tm == 0 and memq_col % MEM_HD == 0
    qblk = memq_col // MEM_HD
    q_specs = [pl.BlockSpec((None, tm, MEM_HD), functools.partial(lambda bi, i, h: (bi, i, qblk + h), h=h))
               for h in range(MEM_HEADS)]
    return pl.pallas_call(
        _mix_out_kernel,
        out_shape=jax.ShapeDtypeStruct((b, s, d), F32),
        grid=(b, s // tm),
        in_specs=[
            pl.BlockSpec((None, tm, d), lambda bi, i: (bi, i, 0)),
            pl.BlockSpec((None, tm, TOK_W), lambda bi, i: (bi, i, 0)),
            *q_specs,
            pl.BlockSpec((None, MEM_LEN, MEM_W), lambda bi, i: (bi, 0, 0)),
            pl.BlockSpec((None, MEM_LEN, MEM_W), lambda bi, i: (bi, 0, 0)),
            pl.BlockSpec((1, MEM_HD), lambda bi, i: (0, 0)),
            pl.BlockSpec((None, d, d), lambda bi, i: (layer, 0, 0)),
        ],
        out_specs=pl.BlockSpec((None, tm, d), lambda bi, i: (bi, i, 0)),
        compiler_params=_params(("parallel", "arbitrary")),
        name="mix_out",
    )(x, tok, proj, proj, proj, proj, km, vm, q_gain.reshape(1, MEM_HD), w_out)


def kernel(x, mem, positions, ffn_norm_g, ffn_w_gate, ffn_w_up, ffn_w_down, mix_norm_g, mem_norm_g,
           w_mem_kv, memq_norm_g, memk_norm_g, w_out, sb_w_in, swa_w_in, swa_q_norm_g, swa_k_norm_g,
           swa_sinks):
    b, s, d = x.shape
    depth = ffn_norm_g.shape[0]
    w_f32 = (ffn_w_gate, ffn_w_up, ffn_w_down)
    w_ffn = (_interleave_gate_up(ffn_w_gate[0, 0], ffn_w_up[0, 0], 512).astype(BF16),
             ffn_w_down[0, 0].astype(BF16))
    w_kv, w_o = w_mem_kv.astype(BF16), w_out.astype(BF16)
    for i in range(depth):
        x2, w_ffn = _ffn(x.reshape(b * s, d), ffn_norm_g[i, 0], w_ffn, w_f32, (i, 1))
        km, vm = _mem_kv(mem, mem_norm_g[i], w_kv, memk_norm_g[i], i)
        j = i // 2
        if i % 2 == 0:
            col_scale = jnp.where(jnp.arange(sb_w_in.shape[2]) < TOK_W, SB_HD ** -0.5, 1.0)
            w_in = (sb_w_in[j] * col_scale).astype(BF16)
            proj = _norm_proj(x2, mix_norm_g[i], w_in, tn=2560).reshape(b, s, -1)
            tok = _sb_attention(proj)
            memq_col = 3 * TOK_W
        else:
            w_in = swa_w_in[j].astype(BF16)
            proj = _norm_proj(x2, mix_norm_g[i], w_in, tn=w_in.shape[1]).reshape(b, s, -1)
            tok = _swa_attention(proj, positions, swa_q_norm_g[j], swa_k_norm_g[j], swa_sinks[j])
            memq_col = TOK_W + 2 * SWA_KV_W
        x3 = _mix_out(x2.reshape(b, s, d), tok, proj, km, vm, memq_norm_g[i], w_o, i,
                      memq_col=memq_col)
        x, w_ffn = _ffn(x3.reshape(b * s, d), ffn_norm_g[i, 1], w_ffn, w_f32,
                        (i + 1, 0) if i + 1 < depth else None)
        x = x.reshape(b, s, d)
    return x
```
